```python
import math
import jax, jax.numpy as jnp
from jax import lax
import numpy as np

D_MODEL = 1024
BATCH = 1
SEQ = 16384
DEPTH = 2

CHUNK = 64
HEAD_DIM = 64
N_RET = 6
N_SB = 5
N_FOX = 5
W_RET = N_RET * HEAD_DIM
W_SB = N_SB * HEAD_DIM
W_FOX = N_FOX * HEAD_DIM
MIX_WIDTH = W_RET + W_SB + W_FOX
IN_WIDTH = 4 * W_RET + 3 * W_SB + 3 * W_FOX + N_FOX
Q_BLOCK = 128
ROPE_BASE = 10000.0
EPS = 1e-6
N_GROUPS = 4
EXPERTS_PER_GROUP = 8
N_EXPERTS = N_GROUPS * EXPERTS_PER_GROUP
TOP_K = 2
D_EXPERT = 512
MOE_BLOCK = 128

kernel_name = "hybrid_retention_stickbreak_fox_hmoe"


def rmsnorm(x, g):
    xf = x.astype(jnp.float32)
    y = xf * lax.rsqrt(jnp.mean(xf * xf, axis=-1, keepdims=True) + EPS)
    return (y * g.astype(jnp.float32)).astype(x.dtype)


def to_heads(t, n):
    B, S, _ = t.shape
    return t.reshape(B, S, n, HEAD_DIM).transpose(0, 2, 1, 3)


def from_heads(t):
    B, H, S, d = t.shape
    return t.transpose(0, 2, 1, 3).reshape(B, S, H * d)


def rotary(x, pos):
    half = HEAD_DIM // 2
    inv = 1.0 / (ROPE_BASE ** (jnp.arange(half, dtype=jnp.float32) / half))
    ang = pos.astype(jnp.float32)[:, None] * inv[None, :]
    cos = jnp.cos(ang).astype(x.dtype)
    sin = jnp.sin(ang).astype(x.dtype)
    x1, x2 = x[..., :half], x[..., half:]
    return jnp.concatenate([x1 * cos - x2 * sin, x1 * sin + x2 * cos], axis=-1)


def head_group_norm(o, g):
    H, d = o.shape[1], o.shape[3]
    mu = jnp.mean(o, axis=-1, keepdims=True)
    c = o - mu
    var = jnp.mean(c * c, axis=-1, keepdims=True)
    return c * lax.rsqrt(var + EPS) * g.astype(jnp.float32).reshape(1, H, 1, d)


def head_rms_norm(o, g):
    H, d = o.shape[1], o.shape[3]
    return o * lax.rsqrt(jnp.mean(o * o, axis=-1, keepdims=True) + EPS) * g.astype(jnp.float32).reshape(1, H, 1, d)


def retention(q, k, v):
    B, H, S, d = q.shape
    N = S // CHUNK
    f32 = jnp.float32
    qc = q.astype(f32).reshape(B, H, N, CHUNK, d)
    kc = k.astype(f32).reshape(B, H, N, CHUNK, d)
    vc = v.astype(f32).reshape(B, H, N, CHUNK, d)
    log_gamma = jnp.log(1.0 - 2.0 ** (-5.0 - jnp.arange(H, dtype=f32)))
    idx = jnp.arange(CHUNK, dtype=f32)
    intra_decay = jnp.exp(log_gamma[:, None, None] * jnp.abs(idx[:, None] - idx[None, :]))
    q_decay = jnp.exp(log_gamma[:, None] * (idx + 1.0))
    k_decay = jnp.exp(log_gamma[:, None] * (CHUNK - 1.0 - idx))
    chunk_decay = jnp.exp(log_gamma * CHUNK)
    scores = jnp.einsum('bhncd,bhnkd->bhnck', qc, kc) * intra_decay[None, :, None]
    o_intra = jnp.einsum('bhnck,bhnke->bhnce', scores, vc)
    u = jnp.einsum('bhnkd,bhnke->bhnde', kc * k_decay[None, :, None, :, None], vc)

    def step(state, u_n):
        return state * chunk_decay[None, :, None, None] + u_n, state

    _, state_prev = lax.scan(step, jnp.zeros((B, H, d, d), f32), jnp.moveaxis(u, 2, 0))
    state_prev = jnp.moveaxis(state_prev, 0, 2)
    o_cross = jnp.einsum('bhncd,bhnde->bhnce', qc * q_decay[None, :, None, :, None], state_prev)
    return (o_intra + o_cross).reshape(B, H, S, d)


def causal_block_mixers(q_sb, k_sb, v_sb, q_fx, k_fx, v_fx, log_f):
    B, H, S, d = q_sb.shape
    nb = S // Q_BLOCK
    scale = d ** -0.5
    f32 = jnp.float32
    F = jnp.cumsum(log_f.astype(f32), axis=-1)
    key_pos = jnp.arange(S)

    def to_blocks(t):
        return jnp.moveaxis(t.reshape(t.shape[:2] + (nb, Q_BLOCK) + t.shape[3:]), 2, 0)

    def one_block(args):
        i, qs, qf, Fq = args
        q_pos = i * Q_BLOCK + jnp.arange(Q_BLOCK)
        z = jnp.einsum('bhqd,bhkd->bhqk', qs, k_sb).astype(f32) * scale
        strict = key_pos[None, :] < q_pos[:, None]
        log_beta = jax.nn.log_sigmoid(z)
        log_1m = jnp.where(strict, log_beta - z, 0.0)
        between = lax.cumsum(log_1m, axis=3, reverse=True) - log_1m
        a = jnp.where(strict, jnp.exp(log_beta + between), 0.0)
        o_sb = jnp.einsum('bhqk,bhkd->bhqd', a.astype(v_sb.dtype), v_sb).astype(f32)
        logits = jnp.einsum('bhqd,bhkd->bhqk', qf, k_fx).astype(f32) * scale
        logits = logits + Fq[..., :, None] - F[..., None, :]
        causal = key_pos[None, :] <= q_pos[:, None]
        p = jax.nn.softmax(jnp.where(causal, logits, -jnp.inf), axis=-1)
        o_fx = jnp.einsum('bhqk,bhkd->bhqd', p.astype(v_fx.dtype), v_fx).astype(f32)
        return o_sb, o_fx

    o_sb, o_fx = lax.map(one_block, (jnp.arange(nb), to_blocks(q_sb), to_blocks(q_fx), to_blocks(F)))
    o_sb = jnp.moveaxis(o_sb, 0, 2).reshape(B, H, S, d)
    o_fx = jnp.moveaxis(o_fx, 0, 2).reshape(B, H, S, d)
    return o_sb, o_fx


def mixer_block(h, w_in, b_forget, g_ret, g_sb, g_fox, w_out, pos):
    f32 = jnp.float32
    proj = h @ w_in
    sizes = [W_RET] * 4 + [W_SB] * 3 + [W_FOX] * 3 + [N_FOX]
    cuts, acc = [], 0
    for s in sizes[:-1]:
        acc += s
        cuts.append(acc)
    q_r, k_r, v_r, gate_r, q_s, k_s, v_s, q_f, k_f, v_f, f_logit = jnp.split(proj, cuts, axis=-1)
    qr = rotary(to_heads(q_r, N_RET), pos) * (HEAD_DIM ** -0.5)
    kr = rotary(to_heads(k_r, N_RET), pos)
    o_ret = head_group_norm(retention(qr, kr, to_heads(v_r, N_RET)), g_ret)
    o_ret = from_heads(o_ret) * jax.nn.silu(gate_r.astype(f32))
    log_f = jax.nn.log_sigmoid(f_logit.astype(f32) + b_forget.astype(f32)).transpose(0, 2, 1)
    o_sb, o_fx = causal_block_mixers(to_heads(q_s, N_SB), to_heads(k_s, N_SB), to_heads(v_s, N_SB),
                                     to_heads(q_f, N_FOX), to_heads(k_f, N_FOX), to_heads(v_f, N_FOX), log_f)
    o_sb = from_heads(head_rms_norm(o_sb, g_sb))
    o_fx = from_heads(head_rms_norm(o_fx, g_fox))
    merged = jnp.concatenate([o_ret, o_sb, o_fx], axis=-1).astype(h.dtype)
    return merged @ w_out


def hier_moe(h, w_group, b_group, w_router, b_router, w_gate, w_up, w_down):
    B, S, D = h.shape
    T = B * S
    f32 = jnp.float32
    xt = h.reshape(T, D)
    g_logits = (xt @ w_group).astype(f32) + b_group.astype(f32)
    g_prob = jax.nn.softmax(g_logits, axis=-1)
    grp = jnp.argmax(g_logits, axis=-1)
    p_grp = jnp.take_along_axis(g_prob, grp[:, None], axis=-1)
    e_logits = ((xt @ w_router).astype(f32) + b_router.astype(f32)).reshape(T, N_GROUPS, EXPERTS_PER_GROUP)
    e_logits = jnp.take_along_axis(e_logits, grp[:, None, None], axis=1)[:, 0]
    top_val, top_idx = lax.top_k(e_logits, TOP_K)
    gate = jax.nn.softmax(top_val, axis=-1) * p_grp
    expert_id = (grp[:, None] * EXPERTS_PER_GROUP + top_idx).reshape(-1)
    A = T * TOP_K
    order = jnp.argsort(expert_id)
    sorted_e = expert_id[order]
    token_of = order // TOP_K
    gate_sorted = gate.reshape(-1)[order]
    counts = jnp.bincount(expert_id, length=N_EXPERTS)
    start = jnp.cumsum(counts) - counts
    padded = ((counts + MOE_BLOCK - 1) // MOE_BLOCK) * MOE_BLOCK
    padded_end = jnp.cumsum(padded)
    padded_start = padded_end - padded
    dest = padded_start[sorted_e] + (jnp.arange(A) - start[sorted_e])
    P = ((A + MOE_BLOCK - 1) // MOE_BLOCK) * MOE_BLOCK + N_EXPERTS * MOE_BLOCK
    nblk = P // MOE_BLOCK
    buf = jnp.zeros((P, D), h.dtype).at[dest].set(xt[token_of])
    blk_start = jnp.arange(nblk) * MOE_BLOCK
    blk_expert = jnp.minimum(jnp.sum(padded_end[None, :] <= blk_start[:, None], axis=-1), N_EXPERTS - 1)

    def expert_block(args):
        xb, e = args
        hid = jax.nn.silu(xb @ w_gate[e]) * (xb @ w_up[e])
        return hid @ w_down[e]

    ybuf = lax.map(expert_block, (buf.reshape(nblk, MOE_BLOCK, D), blk_expert)).reshape(P, D)
    y = ybuf[dest] * gate_sorted[:, None].astype(ybuf.dtype)
    out = jax.ops.segment_sum(y, token_of, num_segments=T)
    return out.reshape(B, S, D)


def setup_inputs(seed: int = 0) -> dict:
    key = jax.random.key(seed)
    ks = jax.random.split(key, 18)
    L, D = DEPTH, D_MODEL

    def nrm(k, shape, scale):
        return jax.random.normal(k, shape, jnp.float32) * scale

    return {
        "x": nrm(ks[0], (BATCH, SEQ, D), 1.0),
        "norm_mix": 1.0 + nrm(ks[1], (L, D), 0.02),
        "w_in": nrm(ks[2], (L, D, IN_WIDTH), D ** -0.5),
        "b_forget": jax.random.uniform(ks[3], (L, N_FOX), jnp.float32, 1.0, 4.0),
        "g_ret": 1.0 + nrm(ks[4], (L, W_RET), 0.02),
        "g_sb": 1.0 + nrm(ks[5], (L, W_SB), 0.02),
        "g_fox": 1.0 + nrm(ks[6], (L, W_FOX), 0.02),
        "w_out": nrm(ks[7], (L, MIX_WIDTH, D), MIX_WIDTH ** -0.5),
        "norm_ffn": 1.0 + nrm(ks[8], (L, D), 0.02),
        "w_group": nrm(ks[9], (L, D, N_GROUPS), D ** -0.5),
        "b_group": nrm(ks[10], (L, N_GROUPS), 0.01),
        "w_router": nrm(ks[11], (L, D, N_EXPERTS), D ** -0.5),
        "b_router": nrm(ks[12], (L, N_EXPERTS), 0.01),
        "w_gate": nrm(ks[13], (L, N_EXPERTS, D, D_EXPERT), D ** -0.5),
        "w_up": nrm(ks[14], (L, N_EXPERTS, D, D_EXPERT), D ** -0.5),
        "w_down": nrm(ks[15], (L, N_EXPERTS, D_EXPERT, D), D_EXPERT ** -0.5),
        "norm_final": 1.0 + nrm(ks[16], (D,), 0.02),
    }


def reference(x, norm_mix, w_in, b_forget, g_ret, g_sb, g_fox, w_out, norm_ffn, w_group, b_group,
              w_router, b_router, w_gate, w_up, w_down, norm_final):
    S = x.shape[1]
    pos = jnp.arange(S)
    for l in range(DEPTH):
        mix = mixer_block(rmsnorm(x, norm_mix[l]), w_in[l], b_forget[l], g_ret[l], g_sb[l], g_fox[l],
                          w_out[l], pos)
        x = x + mix.astype(x.dtype)
        ffn = hier_moe(rmsnorm(x, norm_ffn[l]), w_group[l], b_group[l], w_router[l], b_router[l],
                       w_gate[l], w_up[l], w_down[l])
        x = x + ffn.astype(x.dtype)
    return rmsnorm(x, norm_final)
```

```python
import functools

import jax
import jax.numpy as jnp
import numpy as np
from jax import lax
from jax.experimental import pallas as pl
from jax.experimental.pallas import tpu as pltpu

F32 = jnp.float32
BF16 = jnp.bfloat16

D_MODEL = 1024
HEAD_DIM = 64
HALF = HEAD_DIM // 2
N_RET, N_SB, N_FOX = 6, 5, 5
PAIRS = 3
GROUP_W = PAIRS * 128
CHUNK = 64
ROPE_BASE = 10000.0
EPS = 1e-6
N_GROUPS, EXPERTS_PER_GROUP = 4, 8
N_EXPERTS = N_GROUPS * EXPERTS_PER_GROUP
TOP_K = 2
D_EXPERT = 512
Q_SCALE = HEAD_DIM ** -0.5

LANES = 128
VMEM_LIMIT = 56 * 1024 * 1024

SEG_QR, SEG_QR_SW, SEG_KR, SEG_KR_SW, SEG_VR, SEG_GATE = 0, 1, 2, 3, 4, 5
SEG_QS, SEG_KS, SEG_VS, SEG_QF, SEG_KF, SEG_VF = 6, 7, 8, 9, 10, 11
N_SEG = 12
PROJ_W = N_SEG * GROUP_W + LANES


def _params(*sem):
    return pltpu.CompilerParams(dimension_semantics=sem, vmem_limit_bytes=VMEM_LIMIT)


def _rmsnorm(x, g):
    return x * lax.rsqrt(jnp.mean(x * x, axis=-1, keepdims=True) + EPS) * g


def _split3(x):
    hi = x.astype(BF16)
    r = x - hi.astype(F32)
    mid = r.astype(BF16)
    lo = (r - mid.astype(F32)).astype(BF16)
    return hi, mid, lo


def _dot(a, b):
    return jnp.dot(a, b, preferred_element_type=F32)


def _dot_nt(a, b):
    return lax.dot_general(a, b, (((1,), (1,)), ((), ())), preferred_element_type=F32)


def _proj_kernel(x_ref, g_ref, w_ref, cos_ref, sin_ref, bf_ref, tri_ref,
                 qr_ref, kr_ref, vr_ref, gate_ref, qs_ref, ks_ref, vs_ref, qf_ref, kf_ref, vf_ref,
                 fcum_ref, carry_ref):
    @pl.when(pl.program_id(0) == 0)
    def _():
        carry_ref[...] = jnp.zeros_like(carry_ref)

    h = _rmsnorm(x_ref[...], g_ref[...]).astype(BF16)

    def seg(k, width=GROUP_W):
        return _dot(h, w_ref[:, k * GROUP_W:k * GROUP_W + width])

    cos = jnp.concatenate([cos_ref[...]] * PAIRS, axis=1)
    sin = jnp.concatenate([sin_ref[...]] * PAIRS, axis=1)
    qr_ref[...] = (seg(SEG_QR) * cos + seg(SEG_QR_SW) * sin).astype(BF16)
    kr_ref[...] = (seg(SEG_KR) * cos + seg(SEG_KR_SW) * sin).astype(BF16)
    vr_ref[...] = seg(SEG_VR).astype(BF16)
    gate = seg(SEG_GATE)
    gate_ref[...] = gate * (1.0 / (1.0 + jnp.exp(-gate)))
    qs_ref[...] = seg(SEG_QS).astype(BF16)
    ks_ref[...] = seg(SEG_KS).astype(BF16)
    vs_ref[...] = seg(SEG_VS).astype(BF16)
    qf_ref[...] = seg(SEG_QF).astype(BF16)
    kf_ref[...] = seg(SEG_KF).astype(BF16)
    vf_ref[...] = seg(SEG_VF).astype(BF16)

    fl = seg(N_SEG, LANES) + bf_ref[...]
    logf = jnp.minimum(fl, 0.0) - jnp.log1p(jnp.exp(-jnp.abs(fl)))
    hi, mid, lo = _split3(logf)
    tri = tri_ref[...]
    fc = _dot(tri, hi) + _dot(tri, mid) + _dot(tri, lo) + carry_ref[...]
    fcum_ref[...] = fc
    carry_ref[...] = fc[-1:, :]


def _proj(x, g, w_all, cos_t, sin_t, bf_pad, ts):
    S = x.shape[0]
    tri = jnp.tril(jnp.ones((ts, ts), F32)).astype(BF16)
    row = lambda w: pl.BlockSpec((ts, w), lambda i: (i, 0))
    const = lambda a: pl.BlockSpec(a.shape, lambda i: (0,) * a.ndim)
    bshape = jax.ShapeDtypeStruct((S, GROUP_W), BF16)
    outs = [bshape, bshape, bshape, jax.ShapeDtypeStruct((S, GROUP_W), F32)] + [bshape] * 6 + [
        jax.ShapeDtypeStruct((S, LANES), F32)]
    return pl.pallas_call(
        _proj_kernel,
        grid=(S // ts,),
        in_specs=[row(D_MODEL), const(g), const(w_all), row(LANES), row(LANES), const(bf_pad), const(tri)],
        out_specs=[row(GROUP_W)] * 10 + [row(LANES)],
        out_shape=outs,
        scratch_shapes=[pltpu.VMEM((1, LANES), F32)],
        compiler_params=_params("arbitrary"),
        name="proj",
    )(x, g, w_all, cos_t, sin_t, bf_pad, tri)


def _half_masks(shape):
    lane = lax.broadcasted_iota(jnp.int32, shape, len(shape) - 1)
    return lane < HEAD_DIM, lane >= HEAD_DIM


def _per_head_mean(x, m0):
    s0 = jnp.sum(jnp.where(m0, x, 0.0), axis=1, keepdims=True)
    s1 = jnp.sum(jnp.where(m0, 0.0, x), axis=1, keepdims=True)
    return jnp.where(m0, s0, s1) * (1.0 / HEAD_DIM)


def _ret_kernel(q_ref, k_ref, v_ref, gate_ref, dmask_ref, kdec_ref, qdec_ref, cdec_ref, g_ref,
                o_ref, state_ref):
    @pl.when(pl.program_id(1) == 0)
    def _():
        state_ref[...] = jnp.zeros_like(state_ref)

    q, k, v = q_ref[...], k_ref[...], v_ref[...]
    m0, m1 = _half_masks(q.shape)
    state = state_ref[...]
    qd = (q.astype(F32) * qdec_ref[0]).astype(BF16)
    o = _dot(qd, state.astype(BF16))
    for hh, m in enumerate((m0, m1)):
        s = _dot_nt(jnp.where(m, q, jnp.zeros_like(q)), k)
        oh = _dot((s * dmask_ref[hh]).astype(BF16), v)
        o = o + jnp.where(m, oh, 0.0)
    kdt = (k.astype(F32) * kdec_ref[0]).T.astype(BF16)
    u = _dot(kdt, v)
    r = lax.broadcasted_iota(jnp.int32, u.shape, 0) < HEAD_DIM
    c = lax.broadcasted_iota(jnp.int32, u.shape, 1) < HEAD_DIM
    state_ref[...] = state * cdec_ref[0] + jnp.where(r == c, u, 0.0)
    cen = o - _per_head_mean(o, m0)
    var = _per_head_mean(cen * cen, m0)
    o_ref[...] = (cen * lax.rsqrt(var + EPS) * g_ref[...] * gate_ref[...]).astype(BF16)


def _ret_tables(B):
    hs = np.arange(N_RET, dtype=np.float64)
    lg = np.log(1.0 - 2.0 ** (-5.0 - hs))
    idx = np.arange(B, dtype=np.float64)
    diff = idx[:, None] - idx[None, :]
    ch = np.arange(B) // CHUNK
    same = ch[:, None] == ch[None, :]
    earlier = ch[None, :] < ch[:, None]
    expo = np.where(same, np.abs(diff), diff)
    dmask = np.where(same | earlier, np.exp(lg[:, None, None] * expo[None]), 0.0)
    kdec = np.exp(lg[:, None] * (B - 1.0 - idx)[None])
    qdec = np.exp(lg[:, None] * (idx + 1.0)[None])
    cdec = np.exp(lg * B)
    lane_head = np.arange(LANES) // HEAD_DIM
    kdec_p = np.stack([kdec[2 * p + lane_head].T for p in range(PAIRS)])
    qdec_p = np.stack([qdec[2 * p + lane_head].T for p in range(PAIRS)])
    cdec_p = np.stack([np.broadcast_to(cdec[2 * p + lane_head][None, :], (LANES, LANES)) for p in range(PAIRS)])
    f = lambda a: jnp.asarray(a, F32)
    return f(dmask), f(kdec_p), f(qdec_p), f(cdec_p)


def _retention(qr, kr, vr, gate, g_pad, B):
    S = qr.shape[0]
    dmask, kdec, qdec, cdec = _ret_tables(B)
    blk = pl.BlockSpec((B, LANES), lambda p, b: (b, p))
    return pl.pallas_call(
        _ret_kernel,
        grid=(PAIRS, S // B),
        in_specs=[blk, blk, blk, blk,
                  pl.BlockSpec((2, B, B), lambda p, b: (p, 0, 0)),
                  pl.BlockSpec((1, B, LANES), lambda p, b: (p, 0, 0)),
                  pl.BlockSpec((1, B, LANES), lambda p, b: (p, 0, 0)),
                  pl.BlockSpec((1, LANES, LANES), lambda p, b: (p, 0, 0)),
                  pl.BlockSpec((1, LANES), lambda p, b: (0, p))],
        out_specs=blk,
        out_shape=jax.ShapeDtypeStruct((S, GROUP_W), BF16),
        scratch_shapes=[pltpu.VMEM((LANES, LANES), F32)],
        compiler_params=_params("arbitrary", "arbitrary"),
        name="retention",
    )(qr, kr, vr, gate, dmask, kdec, qdec, cdec, g_pad)


def _head_rms_out(acc0, acc1, g, m0):
    o = jnp.where(m0, acc0, acc1)
    ms = _per_head_mean(o * o, m0)
    return (o * lax.rsqrt(ms + EPS) * g).astype(BF16)


def _sb_kernel(q_ref, k_ref, v_ref, tri_ref, g_ref, o_ref, acc_ref, *, n_heads, tq):
    p, i = pl.program_id(0), pl.program_id(1)
    m0, m1 = _half_masks((tq, LANES))
    acc_ref[...] = jnp.zeros_like(acc_ref)
    row = lax.broadcasted_iota(jnp.int32, (tq, tq), 0)
    col = lax.broadcasted_iota(jnp.int32, (tq, tq), 1)

    def head(hh, m):
        q = q_ref[...]
        qm = jnp.where(m, q, jnp.zeros_like(q))

        def step(j, carry, diag):
            ks = k_ref[pl.ds(pl.multiple_of(j * tq, tq), tq), :]
            vs = v_ref[pl.ds(pl.multiple_of(j * tq, tq), tq), :]
            z = _dot_nt(qm, ks)
            l1m = -(jnp.maximum(z, 0.0) + jnp.log(1.0 + jnp.exp(-jnp.abs(z))))
            if diag:
                l1m = jnp.where(col < row, l1m, 0.0)
            hi = l1m.astype(BF16)
            lo = (l1m - hi.astype(F32)).astype(BF16)
            tri = tri_ref[...]
            csum = _dot(hi, tri) + _dot(lo, tri) + carry
            a = jnp.exp(z + csum)
            if diag:
                a = jnp.where(col < row, a, 0.0)
            acc_ref[hh] += _dot(a.astype(BF16), vs)
            return csum[:, 0:1]

        carry = step(i, jnp.zeros((tq, 1), F32), True)
        lax.fori_loop(0, i, lambda t, c: step(i - 1 - t, c, False), carry)

    head(0, m0)

    @pl.when(2 * p + 1 < n_heads)
    def _():
        head(1, m1)

    o_ref[...] = _head_rms_out(acc_ref[0], acc_ref[1], g_ref[...], m0)


def _stick_breaking(q, k, v, g_pad, tq):
    S = q.shape[0]
    tri = jnp.tril(jnp.ones((tq, tq), F32)).astype(BF16)
    return pl.pallas_call(
        functools.partial(_sb_kernel, n_heads=N_SB, tq=tq),
        grid=(PAIRS, S // tq),
        in_specs=[pl.BlockSpec((tq, LANES), lambda p, i: (i, p)),
                  pl.BlockSpec((S, LANES), lambda p, i: (0, p)),
                  pl.BlockSpec((S, LANES), lambda p, i: (0, p)),
                  pl.BlockSpec((tq, tq), lambda p, i: (0, 0)),
                  pl.BlockSpec((1, LANES), lambda p, i: (0, p))],
        out_specs=pl.BlockSpec((tq, LANES), lambda p, i: (i, p)),
        out_shape=jax.ShapeDtypeStruct((S, GROUP_W), BF16),
        scratch_shapes=[pltpu.VMEM((2, tq, LANES), F32)],
        compiler_params=_params("arbitrary", "arbitrary"),
        name="stick_breaking",
    )(q, k, v, tri, g_pad)


def _fox_kernel(q_ref, k_ref, v_ref, f_ref, g_ref, o_ref, acc_ref, *, n_heads, tq):
    p, i = pl.program_id(0), pl.program_id(1)
    m0, m1 = _half_masks((tq, LANES))
    acc_ref[...] = jnp.zeros_like(acc_ref)
    row = lax.broadcasted_iota(jnp.int32, (tq, tq), 0)
    col = lax.broadcasted_iota(jnp.int32, (tq, tq), 1)

    def head(hh, m):
        q = q_ref[...]
        qm = jnp.where(m, q, jnp.zeros_like(q))
        fref = f_ref[0, i, hh:hh + 1, 0:1]

        def step(j, m_run, diag):
            ks = k_ref[pl.ds(pl.multiple_of(j * tq, tq), tq), :]
            vs = v_ref[pl.ds(pl.multiple_of(j * tq, tq), tq), :]
            t = _dot_nt(qm, ks) + (fref - f_ref[0, j, hh:hh + 1, :])
            if diag:
                t = jnp.where(col <= row, t, -jnp.inf)
            m_new = jnp.maximum(m_run, jnp.max(t, axis=1, keepdims=True))
            pr = jnp.exp(t - m_new)
            vs1 = jnp.where(m, vs, jnp.ones_like(vs))
            acc_ref[hh] = jnp.exp(m_run - m_new) * acc_ref[hh] + _dot(pr.astype(BF16), vs1)
            return m_new

        m_run = step(i, jnp.full((tq, 1), -jnp.inf, F32), True)
        lax.fori_loop(0, i, lambda t, c: step(i - 1 - t, c, False), m_run)
        acc = acc_ref[hh]
        denom = jnp.max(jnp.where(m, -jnp.inf, acc), axis=1, keepdims=True)
        acc_ref[hh] = acc / denom

    head(0, m0)

    @pl.when(2 * p + 1 < n_heads)
    def _():
        head(1, m1)

    o_ref[...] = _head_rms_out(acc_ref[0], acc_ref[1], g_ref[...], m0)


def _fox(q, k, v, f_pairs, g_pad, tq):
    S = q.shape[0]
    nq = S // tq
    return pl.pallas_call(
        functools.partial(_fox_kernel, n_heads=N_FOX, tq=tq),
        grid=(PAIRS, nq),
        in_specs=[pl.BlockSpec((tq, LANES), lambda p, i: (i, p)),
                  pl.BlockSpec((S, LANES), lambda p, i: (0, p)),
                  pl.BlockSpec((S, LANES), lambda p, i: (0, p)),
                  pl.BlockSpec((1, nq, 2, tq), lambda p, i: (p, 0, 0, 0)),
                  pl.BlockSpec((1, LANES), lambda p, i: (0, p))],
        out_specs=pl.BlockSpec((tq, LANES), lambda p, i: (i, p)),
        out_shape=jax.ShapeDtypeStruct((S, GROUP_W), BF16),
        scratch_shapes=[pltpu.VMEM((2, tq, LANES), F32)],
        compiler_params=_params("arbitrary", "arbitrary"),
        name="fox",
    )(q, k, v, f_pairs, g_pad)


def _out_router_kernel(x_ref, ret_ref, sb_ref, fx_ref, w_ref, g_ref, whi_ref, wlo_ref, b_ref,
                       x1_ref, idx_ref, gate_ref):
    mix = _dot(ret_ref[...], w_ref[0]) + _dot(sb_ref[...], w_ref[1]) + _dot(fx_ref[...], w_ref[2])
    x1 = x_ref[...] + mix
    x1_ref[...] = x1
    h = _rmsnorm(x1, g_ref[...])
    hi = h.astype(BF16)
    lo = (h - hi.astype(F32)).astype(BF16)
    whi = whi_ref[...]
    logits = _dot(hi, whi) + _dot(lo, whi) + _dot(hi, wlo_ref[...]) + b_ref[...]
    lane = lax.broadcasted_iota(jnp.int32, logits.shape, 1).astype(F32)
    neg = -jnp.inf
    big = float(LANES)

    def top(vals):
        v = jnp.max(vals, axis=1, keepdims=True)
        at = jnp.min(jnp.where(vals == v, lane, big), axis=1, keepdims=True)
        return v, at

    lg = jnp.where(lane < N_GROUPS, logits, neg)
    gmax, grp = top(lg)
    p_grp = 1.0 / jnp.sum(jnp.exp(lg - gmax), axis=1, keepdims=True)
    lo_lane = N_GROUPS + EXPERTS_PER_GROUP * grp
    le = jnp.where((lane >= lo_lane) & (lane < lo_lane + EXPERTS_PER_GROUP), logits, neg)
    v1, i1 = top(le)
    v2, i2 = top(jnp.where(lane == i1, neg, le))
    e21 = jnp.exp(v2 - v1)
    g1 = p_grp / (1.0 + e21)
    g2 = p_grp * e21 / (1.0 + e21)
    idx = jnp.where(lane == 0.0, i1, i2) - float(N_GROUPS)
    idx_ref[...] = jnp.where(lane < TOP_K, idx, 0.0).astype(jnp.int32)
    gate_ref[...] = jnp.where(lane == 0.0, g1, jnp.where(lane == 1.0, g2, 0.0))


def _out_router(x, o_ret, o_sb, o_fx, w_out3, g, w_hi, w_lo, b_rt, ts):
    S = x.shape[0]
    row = lambda w: pl.BlockSpec((ts, w), lambda i: (i, 0))
    const = lambda a: pl.BlockSpec(a.shape, lambda i: (0,) * a.ndim)
    return pl.pallas_call(
        _out_router_kernel,
        grid=(S // ts,),
        in_specs=[row(D_MODEL), row(GROUP_W), row(GROUP_W), row(GROUP_W), const(w_out3), const(g),
                  const(w_hi), const(w_lo), const(b_rt)],
        out_specs=[row(D_MODEL), row(LANES), row(LANES)],
        out_shape=[jax.ShapeDtypeStruct((S, D_MODEL), F32), jax.ShapeDtypeStruct((S, LANES), jnp.int32),
                   jax.ShapeDtypeStruct((S, LANES), F32)],
        compiler_params=_params("arbitrary"),
        name="out_router",
    )(x, o_ret, o_sb, o_fx, w_out3, g, w_hi, w_lo, b_rt)


def _row_gather(src_hbm, dst, sem, idx_ref, base, n):
    def copy(r):
        return pltpu.make_async_copy(src_hbm.at[pl.ds(idx_ref[base + r], 1), :], dst.at[pl.ds(r, 1), :], sem)

    def start():
        lax.fori_loop(0, n, lambda r, c: (copy(r).start(), c)[1], 0)

    def wait():
        lax.fori_loop(0, n, lambda r, c: (copy(r).wait(), c)[1], 0)

    return start, wait


def _expert_kernel(be_ref, tok_ref, nused_ref, x_hbm, g_ref, wg_ref, wu_ref, wd_ref, y_ref, xbuf, sem, *, rb):
    b = pl.program_id(0)
    nused = nused_ref[0]
    slot = b % 2

    def gather(blk, s):
        return _row_gather(x_hbm, xbuf.at[s], sem.at[s], tok_ref, blk * rb, rb)

    @pl.when(b == 0)
    def _():
        gather(0, 0)[0]()

    @pl.when(b + 1 < nused)
    def _():
        gather(b + 1, 1 - slot)[0]()

    @pl.when(b < nused)
    def _():
        gather(b, slot)[1]()
        h = _rmsnorm(xbuf[slot], g_ref[...]).astype(BF16)
        gt = _dot(h, wg_ref[0].astype(BF16))
        up = _dot(h, wu_ref[0].astype(BF16))
        hid = gt * (1.0 / (1.0 + jnp.exp(-gt))) * up
        y_ref[...] = _dot(hid.astype(BF16), wd_ref[0].astype(BF16))

    @pl.when(b >= nused)
    def _():
        y_ref[...] = jnp.zeros_like(y_ref)


def _experts(blk_expert, tok_slot, nused, x1, g, w_gate, w_up, w_down, rb):
    P = tok_slot.shape[0]
    nblk = P // rb
    live = lambda b, nu: jnp.minimum(b, nu[0] - 1)
    grid_spec = pltpu.PrefetchScalarGridSpec(
        num_scalar_prefetch=3,
        grid=(nblk,),
        in_specs=[pl.BlockSpec(memory_space=pl.ANY),
                  pl.BlockSpec((1, D_MODEL), lambda b, be, tk, nu: (0, 0)),
                  pl.BlockSpec((1, D_MODEL, D_EXPERT), lambda b, be, tk, nu: (be[live(b, nu)], 0, 0)),
                  pl.BlockSpec((1, D_MODEL, D_EXPERT), lambda b, be, tk, nu: (be[live(b, nu)], 0, 0)),
                  pl.BlockSpec((1, D_EXPERT, D_MODEL), lambda b, be, tk, nu: (be[live(b, nu)], 0, 0))],
        out_specs=pl.BlockSpec((rb, D_MODEL), lambda b, be, tk, nu: (b, 0)),
        scratch_shapes=[pltpu.VMEM((2, rb, D_MODEL), F32), pltpu.SemaphoreType.DMA((2,))],
    )
    return pl.pallas_call(
        functools.partial(_expert_kernel, rb=rb),
        grid_spec=grid_spec,
        out_shape=jax.ShapeDtypeStruct((P, D_MODEL), F32),
        compiler_params=_params("arbitrary"),
        name="experts",
    )(blk_expert, tok_slot, nused, x1, g, w_gate, w_up, w_down)


def _combine_kernel(dest_ref, x_ref, gate_ref, y_hbm, gf_ref, o_ref, ybuf, sem, *, tb, final):
    b = pl.program_id(0)
    nb = pl.num_programs(0)
    slot = b % 2

    def gather(blk, s):
        return _row_gather(y_hbm, ybuf.at[s], sem.at[s], dest_ref, blk * (TOP_K * tb), TOP_K * tb)

    @pl.when(b == 0)
    def _():
        gather(0, 0)[0]()

    @pl.when(b + 1 < nb)
    def _():
        gather(b + 1, 1 - slot)[0]()

    gather(b, slot)[1]()
    gate = gate_ref[...]
    y = ybuf[slot]
    out = x_ref[...] + (gate[:, 0:1] * y[:tb] + gate[:, 1:2] * y[tb:])
    if final:
        out = _rmsnorm(out, gf_ref[...])
    o_ref[...] = out


def _combine(dest_blocked, x1, gates, ybuf, g_final, tb, final):
    S = x1.shape[0]
    grid_spec = pltpu.PrefetchScalarGridSpec(
        num_scalar_prefetch=1,
        grid=(S // tb,),
        in_specs=[pl.BlockSpec((tb, D_MODEL), lambda b, d: (b, 0)),
                  pl.BlockSpec((tb, LANES), lambda b, d: (b, 0)),
                  pl.BlockSpec(memory_space=pl.ANY),
                  pl.BlockSpec((1, D_MODEL), lambda b, d: (0, 0))],
        out_specs=pl.BlockSpec((tb, D_MODEL), lambda b, d: (b, 0)),
        scratch_shapes=[pltpu.VMEM((2, TOP_K * tb, D_MODEL), F32), pltpu.SemaphoreType.DMA((2,))],
    )
    return pl.pallas_call(
        functools.partial(_combine_kernel, tb=tb, final=final),
        grid_spec=grid_spec,
        out_shape=jax.ShapeDtypeStruct((S, D_MODEL), F32),
        compiler_params=_params("arbitrary"),
        name="combine",
    )(dest_blocked, x1, gates, ybuf, g_final)


def _pad_cols(w, width):
    return jnp.pad(w, ((0, 0), (0, width - w.shape[1])))


def _swap_cols(w, n_heads):
    w4 = w.reshape(w.shape[0], n_heads, 2, HALF)
    return jnp.stack([-w4[:, :, 1], w4[:, :, 0]], axis=2).reshape(w.shape)


def _proj_weights(w_in):
    w_ret, w_sb, w_fox = N_RET * HEAD_DIM, N_SB * HEAD_DIM, N_FOX * HEAD_DIM
    sizes = [w_ret] * 4 + [w_sb] * 3 + [w_fox] * 3 + [N_FOX]
    offs = np.concatenate([[0], np.cumsum(sizes)])
    q_r, k_r, v_r, gate_r, q_s, k_s, v_s, q_f, k_f, v_f, f_l = [w_in[:, offs[n]:offs[n + 1]] for n in range(11)]
    q_r = q_r * Q_SCALE
    segs = [q_r, _swap_cols(q_r, N_RET), k_r, _swap_cols(k_r, N_RET), v_r, gate_r,
            q_s * Q_SCALE, k_s, v_s, q_f * Q_SCALE, k_f, v_f]
    cols = [_pad_cols(s, GROUP_W) for s in segs] + [_pad_cols(f_l, LANES)]
    return jnp.concatenate(cols, axis=1).astype(BF16)


def _rope_tables(S):
    inv = 1.0 / (ROPE_BASE ** (jnp.arange(HALF, dtype=F32) / HALF))
    ang = jnp.arange(S).astype(F32)[:, None] * inv[None, :]
    cos, sin = jnp.cos(ang), jnp.sin(ang)
    return jnp.concatenate([cos] * 4, axis=1), jnp.concatenate([sin] * 4, axis=1)


def _routing_tables(idx, rb):
    T = idx.shape[0]
    A = T * TOP_K
    P = A + N_EXPERTS * rb
    e = idx[:, :TOP_K].reshape(A)
    onehot = (e[:, None] == jnp.arange(N_EXPERTS, dtype=jnp.int32)[None, :]).astype(jnp.int32)
    cs = jnp.cumsum(onehot, axis=0)
    rank = jnp.sum(cs * onehot, axis=1) - 1
    counts = cs[-1]
    padded = ((counts + rb - 1) // rb) * rb
    pend = jnp.cumsum(padded)
    pstart = pend - padded
    dest = (jnp.sum(onehot * pstart[None, :], axis=1) + rank).astype(jnp.int32)
    tok_slot = jnp.zeros((P,), jnp.int32).at[dest].set(jnp.arange(A, dtype=jnp.int32) // TOP_K)
    blk_start = jnp.arange(P // rb, dtype=jnp.int32) * rb
    blk_expert = jnp.minimum(jnp.sum((pend[None, :] <= blk_start[:, None]).astype(jnp.int32), axis=1),
                             N_EXPERTS - 1).astype(jnp.int32)
    nused = (pend[-1] // rb).astype(jnp.int32).reshape(1)
    return dest, tok_slot, blk_expert, nused


def kernel(x, norm_mix, w_in, b_forget, g_ret, g_sb, g_fox, w_out, norm_ffn, w_group, b_group,
           w_router, b_router, w_gate, w_up, w_down, norm_final):
    S = x.shape[1]
    ts = min(512, S)
    tq = min(256, S)
    rb = 256
    tb = min(256, S)
    depth = w_in.shape[0]
    xs = x.reshape(S, D_MODEL)
    cos_t, sin_t = _rope_tables(S)
    row1 = lambda v, width: jnp.pad(v, (0, width - v.shape[0])).reshape(1, width)
    nq = S // tq

    for l in range(depth):
        w_all = _proj_weights(w_in[l])
        qr, kr, vr, gate, qs, ks, vs, qf, kf, vf, fcum = _proj(
            xs, norm_mix[l].reshape(1, D_MODEL), w_all, cos_t, sin_t, row1(b_forget[l], LANES), ts)
        o_ret = _retention(qr, kr, vr, gate, row1(g_ret[l], GROUP_W), tq)
        o_sb = _stick_breaking(qs, ks, vs, row1(g_sb[l], GROUP_W), tq)
        f_pairs = fcum[:, :2 * PAIRS].T.reshape(PAIRS, 2, nq, tq).transpose(0, 2, 1, 3)
        o_fx = _fox(qf, kf, vf, f_pairs, row1(g_fox[l], GROUP_W), tq)

        w_o = w_out[l]
        w_ret, w_sb = N_RET * HEAD_DIM, N_SB * HEAD_DIM
        pad_rows = lambda w: jnp.pad(w, ((0, GROUP_W - w.shape[0]), (0, 0)))
        w_out3 = jnp.stack([pad_rows(w_o[:w_ret]), pad_rows(w_o[w_ret:w_ret + w_sb]),
                            pad_rows(w_o[w_ret + w_sb:])]).astype(BF16)
        w_rt = _pad_cols(jnp.concatenate([w_group[l], w_router[l]], axis=1), LANES)
        w_hi = w_rt.astype(BF16)
        w_lo = (w_rt - w_hi.astype(F32)).astype(BF16)
        b_rt = row1(jnp.concatenate([b_group[l], b_router[l]]), LANES)
        g_ffn = norm_ffn[l].reshape(1, D_MODEL)
        x1, ridx, rgate = _out_router(xs, o_ret, o_sb, o_fx, w_out3, g_ffn, w_hi, w_lo, b_rt, ts)

        dest, tok_slot, blk_expert, nused = _routing_tables(ridx, rb)
        ybuf = _experts(blk_expert, tok_slot, nused, x1, g_ffn, w_gate[l], w_up[l], w_down[l], rb)
        dest_blocked = dest.reshape(S // tb, tb, TOP_K).transpose(0, 2, 1).reshape(-1)
        xs = _combine(dest_blocked, x1, rgate, ybuf, norm_final.reshape(1, D_MODEL), tb, l == depth - 1)
    return xs.reshape(x.shape)
```

```python
import functools

import jax
import jax.numpy as jnp
import numpy as np
from jax import lax
from jax.experimental import pallas as pl
from jax.experimental.pallas import tpu as pltpu

F32 = jnp.float32
BF16 = jnp.bfloat16

D_MODEL = 1024
HEAD_DIM = 64
HALF = HEAD_DIM // 2
N_RET, N_SB, N_FOX = 6, 5, 5
PAIRS = 3
GROUP_W = PAIRS * 128
CHUNK = 64
ROPE_BASE = 10000.0
EPS = 1e-6
N_GROUPS, EXPERTS_PER_GROUP = 4, 8
N_EXPERTS = N_GROUPS * EXPERTS_PER_GROUP
TOP_K = 2
D_EXPERT = 512
Q_SCALE = HEAD_DIM ** -0.5

EXP_ZERO_BELOW = -110.0
BOUND_SLACK = 1.0 + 2.0 ** -10

LANES = 128
VMEM_LIMIT = 56 * 1024 * 1024

SEG_QR, SEG_QR_SW, SEG_KR, SEG_KR_SW, SEG_VR, SEG_GATE = 0, 1, 2, 3, 4, 5
SEG_QS, SEG_KS, SEG_VS, SEG_QF, SEG_KF, SEG_VF = 6, 7, 8, 9, 10, 11
N_SEG = 12
PROJ_W = N_SEG * GROUP_W + LANES


def _params(*sem):
    return pltpu.CompilerParams(dimension_semantics=sem, vmem_limit_bytes=VMEM_LIMIT)


def _rmsnorm(x, g):
    return x * lax.rsqrt(jnp.mean(x * x, axis=-1, keepdims=True) + EPS) * g


def _split3(x):
    hi = x.astype(BF16)
    r = x - hi.astype(F32)
    mid = r.astype(BF16)
    lo = (r - mid.astype(F32)).astype(BF16)
    return hi, mid, lo


def _dot(a, b):
    return jnp.dot(a, b, preferred_element_type=F32)


def _dot_nt(a, b):
    return lax.dot_general(a, b, (((1,), (1,)), ((), ())), preferred_element_type=F32)


def _proj_kernel(x_ref, g_ref, w_ref, cos_ref, sin_ref, bf_ref, tri_ref,
                 qr_ref, kr_ref, vr_ref, gate_ref, qs_ref, ks_ref, vs_ref, qf_ref, kf_ref, vf_ref,
                 fcum_ref, carry_ref):
    @pl.when(pl.program_id(0) == 0)
    def _():
        carry_ref[...] = jnp.zeros_like(carry_ref)

    h = _rmsnorm(x_ref[...], g_ref[...]).astype(BF16)

    def seg(k, width=GROUP_W):
        return _dot(h, w_ref[:, k * GROUP_W:k * GROUP_W + width])

    cos = jnp.concatenate([cos_ref[...]] * PAIRS, axis=1)
    sin = jnp.concatenate([sin_ref[...]] * PAIRS, axis=1)
    qr_ref[...] = (seg(SEG_QR) * cos + seg(SEG_QR_SW) * sin).astype(BF16)
    kr_ref[...] = (seg(SEG_KR) * cos + seg(SEG_KR_SW) * sin).astype(BF16)
    vr_ref[...] = seg(SEG_VR).astype(BF16)
    gate = seg(SEG_GATE)
    gate_ref[...] = gate * (1.0 / (1.0 + jnp.exp(-gate)))
    qs_ref[...] = seg(SEG_QS).astype(BF16)
    ks_ref[...] = seg(SEG_KS).astype(BF16)
    vs_ref[...] = seg(SEG_VS).astype(BF16)
    qf_ref[...] = seg(SEG_QF).astype(BF16)
    kf_ref[...] = seg(SEG_KF).astype(BF16)
    vf_ref[...] = seg(SEG_VF).astype(BF16)

    fl = seg(N_SEG, LANES) + bf_ref[...]
    logf = jnp.minimum(fl, 0.0) - jnp.log1p(jnp.exp(-jnp.abs(fl)))
    hi, mid, lo = _split3(logf)
    tri = tri_ref[...]
    fc = _dot(tri, hi) + _dot(tri, mid) + _dot(tri, lo) + carry_ref[...]
    fcum_ref[...] = fc
    carry_ref[...] = fc[-1:, :]


def _proj(x, g, w_all, cos_t, sin_t, bf_pad, ts):
    S = x.shape[0]
    tri = jnp.tril(jnp.ones((ts, ts), F32)).astype(BF16)
    row = lambda w: pl.BlockSpec((ts, w), lambda i: (i, 0))
    const = lambda a: pl.BlockSpec(a.shape, lambda i: (0,) * a.ndim)
    bshape = jax.ShapeDtypeStruct((S, GROUP_W), BF16)
    outs = [bshape, bshape, bshape, jax.ShapeDtypeStruct((S, GROUP_W), F32)] + [bshape] * 6 + [
        jax.ShapeDtypeStruct((S, LANES), F32)]
    return pl.pallas_call(
        _proj_kernel,
        grid=(S // ts,),
        in_specs=[row(D_MODEL), const(g), const(w_all), row(LANES), row(LANES), const(bf_pad), const(tri)],
        out_specs=[row(GROUP_W)] * 10 + [row(LANES)],
        out_shape=outs,
        scratch_shapes=[pltpu.VMEM((1, LANES), F32)],
        compiler_params=_params("arbitrary"),
        name="proj",
    )(x, g, w_all, cos_t, sin_t, bf_pad, tri)


def _half_masks(shape):
    lane = lax.broadcasted_iota(jnp.int32, shape, len(shape) - 1)
    return lane < HEAD_DIM, lane >= HEAD_DIM


def _per_head_mean(x, m0):
    s0 = jnp.sum(jnp.where(m0, x, 0.0), axis=1, keepdims=True)
    s1 = jnp.sum(jnp.where(m0, 0.0, x), axis=1, keepdims=True)
    return jnp.where(m0, s0, s1) * (1.0 / HEAD_DIM)


def _ret_kernel(q_ref, k_ref, v_ref, gate_ref, dmask_ref, kdec_ref, qdec_ref, cdec_ref, g_ref,
                o_ref, state_ref):
    @pl.when(pl.program_id(1) == 0)
    def _():
        state_ref[...] = jnp.zeros_like(state_ref)

    q, k, v = q_ref[...], k_ref[...], v_ref[...]
    m0, m1 = _half_masks(q.shape)
    state = state_ref[...]
    qd = (q.astype(F32) * qdec_ref[0]).astype(BF16)
    o = _dot(qd, state.astype(BF16))
    for hh, m in enumerate((m0, m1)):
        s = _dot_nt(jnp.where(m, q, jnp.zeros_like(q)), k)
        oh = _dot((s * dmask_ref[hh]).astype(BF16), v)
        o = o + jnp.where(m, oh, 0.0)
    kdt = (k.astype(F32) * kdec_ref[0]).T.astype(BF16)
    u = _dot(kdt, v)
    r = lax.broadcasted_iota(jnp.int32, u.shape, 0) < HEAD_DIM
    c = lax.broadcasted_iota(jnp.int32, u.shape, 1) < HEAD_DIM
    state_ref[...] = state * cdec_ref[0] + jnp.where(r == c, u, 0.0)
    cen = o - _per_head_mean(o, m0)
    var = _per_head_mean(cen * cen, m0)
    o_ref[...] = (cen * lax.rsqrt(var + EPS) * g_ref[...] * gate_ref[...]).astype(BF16)


def _ret_tables(B):
    hs = np.arange(N_RET, dtype=np.float64)
    lg = np.log(1.0 - 2.0 ** (-5.0 - hs))
    idx = np.arange(B, dtype=np.float64)
    diff = idx[:, None] - idx[None, :]
    ch = np.arange(B) // CHUNK
    same = ch[:, None] == ch[None, :]
    earlier = ch[None, :] < ch[:, None]
    expo = np.where(same, np.abs(diff), diff)
    dmask = np.where(same | earlier, np.exp(lg[:, None, None] * expo[None]), 0.0)
    kdec = np.exp(lg[:, None] * (B - 1.0 - idx)[None])
    qdec = np.exp(lg[:, None] * (idx + 1.0)[None])
    cdec = np.exp(lg * B)
    lane_head = np.arange(LANES) // HEAD_DIM
    kdec_p = np.stack([kdec[2 * p + lane_head].T for p in range(PAIRS)])
    qdec_p = np.stack([qdec[2 * p + lane_head].T for p in range(PAIRS)])
    cdec_p = np.stack([np.broadcast_to(cdec[2 * p + lane_head][None, :], (LANES, LANES)) for p in range(PAIRS)])
    f = lambda a: jnp.asarray(a, F32)
    return f(dmask), f(kdec_p), f(qdec_p), f(cdec_p)


def _retention(qr, kr, vr, gate, g_pad, B):
    S = qr.shape[0]
    dmask, kdec, qdec, cdec = _ret_tables(B)
    blk = pl.BlockSpec((B, LANES), lambda p, b: (b, p))
    return pl.pallas_call(
        _ret_kernel,
        grid=(PAIRS, S // B),
        in_specs=[blk, blk, blk, blk,
                  pl.BlockSpec((2, B, B), lambda p, b: (p, 0, 0)),
                  pl.BlockSpec((1, B, LANES), lambda p, b: (p, 0, 0)),
                  pl.BlockSpec((1, B, LANES), lambda p, b: (p, 0, 0)),
                  pl.BlockSpec((1, LANES, LANES), lambda p, b: (p, 0, 0)),
                  pl.BlockSpec((1, LANES), lambda p, b: (0, p))],
        out_specs=blk,
        out_shape=jax.ShapeDtypeStruct((S, GROUP_W), BF16),
        scratch_shapes=[pltpu.VMEM((LANES, LANES), F32)],
        compiler_params=_params("arbitrary", "arbitrary"),
        name="retention",
    )(qr, kr, vr, gate, dmask, kdec, qdec, cdec, g_pad)


def _head_rms_out(acc0, acc1, g, m0):
    o = jnp.where(m0, acc0, acc1)
    ms = _per_head_mean(o * o, m0)
    return (o * lax.rsqrt(ms + EPS) * g).astype(BF16)


def _pair_rows(q, m0):
    zero = jnp.zeros_like(q)
    return jnp.concatenate([jnp.where(m0, q, zero), jnp.where(m0, zero, q)], axis=0)


def _rows2(a0, a1, tq):
    w = a0.shape[1]
    return jnp.concatenate([jnp.broadcast_to(a0, (tq, w)), jnp.broadcast_to(a1, (tq, w))], axis=0)


def _max_key_norm(k_ref, kmax_ref, tq, m0):
    def blk(j, mx):
        kk = k_ref[pl.ds(pl.multiple_of(j * tq, tq), tq), :].astype(F32)
        n2 = _per_head_mean(kk * kk, m0) * float(HEAD_DIM)
        return jnp.maximum(mx, jnp.max(n2, axis=0, keepdims=True))

    n2max = lax.fori_loop(0, k_ref.shape[0] // tq, blk, jnp.zeros((1, LANES), F32))
    kmax_ref[...] = jnp.sqrt(n2max)


def _score_bound(qm, kmax_ref, tq):
    qf = qm.astype(F32)
    qn = jnp.sqrt(jnp.sum(qf * qf, axis=1, keepdims=True))
    kn = _rows2(kmax_ref[:, 0:1], kmax_ref[:, HEAD_DIM:HEAD_DIM + 1], tq)
    return qn * kn * BOUND_SLACK


def _sb_kernel(q_ref, k_ref, v_ref, tri_ref, g_ref, o_ref, acc_ref, kmax_ref, *, tq):
    i = pl.program_id(1)
    m0, _ = _half_masks((tq, LANES))

    @pl.when(i == 0)
    def _():
        _max_key_norm(k_ref, kmax_ref, tq, m0)

    qm = _pair_rows(q_ref[...], m0)
    zbound = _score_bound(qm, kmax_ref, tq)
    row = lax.broadcasted_iota(jnp.int32, (2 * tq, tq), 0) & (tq - 1)
    col = lax.broadcasted_iota(jnp.int32, (2 * tq, tq), 1)
    acc_ref[...] = jnp.zeros_like(acc_ref)

    def step(j, carry, diag):
        ks = k_ref[pl.ds(pl.multiple_of(j * tq, tq), tq), :]
        vs = v_ref[pl.ds(pl.multiple_of(j * tq, tq), tq), :]
        z = _dot_nt(qm, ks)
        l1m = -(jnp.maximum(z, 0.0) + jnp.log(1.0 + jnp.exp(-jnp.abs(z))))
        if diag:
            l1m = jnp.where(col < row, l1m, 0.0)
        hi = l1m.astype(BF16)
        lo = (l1m - hi.astype(F32)).astype(BF16)
        tri = tri_ref[...]
        csum = _dot(hi, tri) + _dot(lo, tri) + carry
        a = jnp.exp(z + csum)
        if diag:
            a = jnp.where(col < row, a, 0.0)
        acc_ref[...] += _dot(a.astype(BF16), vs)
        return csum[:, 0:1]

    def alive(carry):
        return jnp.max(carry + zbound) >= EXP_ZERO_BELOW

    def body(st):
        j, carry, _ = st
        carry = step(j, carry, False)
        return j - 1, carry, alive(carry)

    carry = step(i, jnp.zeros((2 * tq, 1), F32), True)
    lax.while_loop(lambda st: jnp.logical_and(st[0] >= 0, st[2]), body, (i - 1, carry, alive(carry)))
    o_ref[...] = _head_rms_out(acc_ref[:tq], acc_ref[tq:], g_ref[...], m0)


def _stick_breaking(q, k, v, g_pad, tq):
    S = q.shape[0]
    tri = jnp.tril(jnp.ones((tq, tq), F32)).astype(BF16)
    return pl.pallas_call(
        functools.partial(_sb_kernel, tq=tq),
        grid=(PAIRS, S // tq),
        in_specs=[pl.BlockSpec((tq, LANES), lambda p, i: (i, p)),
                  pl.BlockSpec((S, LANES), lambda p, i: (0, p)),
                  pl.BlockSpec((S, LANES), lambda p, i: (0, p)),
                  pl.BlockSpec((tq, tq), lambda p, i: (0, 0)),
                  pl.BlockSpec((1, LANES), lambda p, i: (0, p))],
        out_specs=pl.BlockSpec((tq, LANES), lambda p, i: (i, p)),
        out_shape=jax.ShapeDtypeStruct((S, GROUP_W), BF16),
        scratch_shapes=[pltpu.VMEM((2 * tq, LANES), F32), pltpu.VMEM((1, LANES), F32)],
        compiler_params=_params("arbitrary", "arbitrary"),
        name="stick_breaking",
    )(q, k, v, tri, g_pad)


def _fox_kernel(q_ref, k_ref, v_ref, f_ref, g_ref, o_ref, acc_ref, kmax_ref, *, tq):
    i = pl.program_id(1)
    m0, _ = _half_masks((tq, LANES))

    @pl.when(i == 0)
    def _():
        _max_key_norm(k_ref, kmax_ref, tq, m0)

    qm = _pair_rows(q_ref[...], m0)
    sbound = _score_bound(qm, kmax_ref, tq)
    row = lax.broadcasted_iota(jnp.int32, (2 * tq, tq), 0) & (tq - 1)
    col = lax.broadcasted_iota(jnp.int32, (2 * tq, tq), 1)
    acc_ref[...] = jnp.zeros_like(acc_ref)

    def frows(j, lanes):
        return _rows2(f_ref[0, j, 0:1, lanes], f_ref[0, j, 1:2, lanes], tq)

    fref = frows(i, slice(0, 1))

    def step(j, m_run, diag):
        ks = k_ref[pl.ds(pl.multiple_of(j * tq, tq), tq), :]
        vs = v_ref[pl.ds(pl.multiple_of(j * tq, tq), tq), :]
        t = _dot_nt(qm, ks) + (fref - frows(j, slice(None)))
        if diag:
            t = jnp.where(col <= row, t, -jnp.inf)
        m_new = jnp.maximum(m_run, jnp.max(t, axis=1, keepdims=True))
        pr = jnp.exp(t - m_new).astype(BF16)
        one = jnp.ones_like(vs)
        pv = jnp.concatenate([_dot(pr[:tq], jnp.where(m0, vs, one)), _dot(pr[tq:], jnp.where(m0, one, vs))], axis=0)
        acc_ref[...] = jnp.exp(m_run - m_new) * acc_ref[...] + pv
        return m_new

    def alive(j, m_run):
        jj = jnp.maximum(j, 0)
        return jnp.max(sbound - m_run + (fref - frows(jj, slice(tq - 1, tq)))) >= EXP_ZERO_BELOW

    def body(st):
        j, m_run, _ = st
        m_run = step(j, m_run, False)
        return j - 1, m_run, alive(j - 1, m_run)

    m_run = step(i, jnp.full((2 * tq, 1), -jnp.inf, F32), True)
    lax.while_loop(lambda st: jnp.logical_and(st[0] >= 0, st[2]), body, (i - 1, m_run, alive(i - 1, m_run)))
    acc0, acc1 = acc_ref[:tq], acc_ref[tq:]
    acc0 = acc0 / jnp.max(jnp.where(m0, -jnp.inf, acc0), axis=1, keepdims=True)
    acc1 = acc1 / jnp.max(jnp.where(m0, acc1, -jnp.inf), axis=1, keepdims=True)
    o_ref[...] = _head_rms_out(acc0, acc1, g_ref[...], m0)


def _fox(q, k, v, f_pairs, g_pad, tq):
    S = q.shape[0]
    nq = S // tq
    return pl.pallas_call(
        functools.partial(_fox_kernel, tq=tq),
        grid=(PAIRS, nq),
        in_specs=[pl.BlockSpec((tq, LANES), lambda p, i: (i, p)),
                  pl.BlockSpec((S, LANES), lambda p, i: (0, p)),
                  pl.BlockSpec((S, LANES), lambda p, i: (0, p)),
                  pl.BlockSpec((1, nq, 2, tq), lambda p, i: (p, 0, 0, 0)),
                  pl.BlockSpec((1, LANES), lambda p, i: (0, p))],
        out_specs=pl.BlockSpec((tq, LANES), lambda p, i: (i, p)),
        out_shape=jax.ShapeDtypeStruct((S, GROUP_W), BF16),
        scratch_shapes=[pltpu.VMEM((2 * tq, LANES), F32), pltpu.VMEM((1, LANES), F32)],
        compiler_params=_params("arbitrary", "arbitrary"),
        name="fox",
    )(q, k, v, f_pairs, g_pad)


def _out_router_kernel(x_ref, ret_ref, sb_ref, fx_ref, w_ref, g_ref, whi_ref, wlo_ref, b_ref,
                       x1_ref, idx_ref, gate_ref):
    mix = _dot(ret_ref[...], w_ref[0]) + _dot(sb_ref[...], w_ref[1]) + _dot(fx_ref[...], w_ref[2])
    x1 = x_ref[...] + mix
    x1_ref[...] = x1
    h = _rmsnorm(x1, g_ref[...])
    hi = h.astype(BF16)
    lo = (h - hi.astype(F32)).astype(BF16)
    whi = whi_ref[...]
    logits = _dot(hi, whi) + _dot(lo, whi) + _dot(hi, wlo_ref[...]) + b_ref[...]
    lane = lax.broadcasted_iota(jnp.int32, logits.shape, 1).astype(F32)
    neg = -jnp.inf
    big = float(LANES)

    def top(vals):
        v = jnp.max(vals, axis=1, keepdims=True)
        at = jnp.min(jnp.where(vals == v, lane, big), axis=1, keepdims=True)
        return v, at

    lg = jnp.where(lane < N_GROUPS, logits, neg)
    gmax, grp = top(lg)
    p_grp = 1.0 / jnp.sum(jnp.exp(lg - gmax), axis=1, keepdims=True)
    lo_lane = N_GROUPS + EXPERTS_PER_GROUP * grp
    le = jnp.where((lane >= lo_lane) & (lane < lo_lane + EXPERTS_PER_GROUP), logits, neg)
    v1, i1 = top(le)
    v2, i2 = top(jnp.where(lane == i1, neg, le))
    e21 = jnp.exp(v2 - v1)
    g1 = p_grp / (1.0 + e21)
    g2 = p_grp * e21 / (1.0 + e21)
    idx = jnp.where(lane == 0.0, i1, i2) - float(N_GROUPS)
    idx_ref[...] = jnp.where(lane < TOP_K, idx, 0.0).astype(jnp.int32)
    gate_ref[...] = jnp.where(lane == 0.0, g1, jnp.where(lane == 1.0, g2, 0.0))


def _out_router(x, o_ret, o_sb, o_fx, w_out3, g, w_hi, w_lo, b_rt, ts):
    S = x.shape[0]
    row = lambda w: pl.BlockSpec((ts, w), lambda i: (i, 0))
    const = lambda a: pl.BlockSpec(a.shape, lambda i: (0,) * a.ndim)
    return pl.pallas_call(
        _out_router_kernel,
        grid=(S // ts,),
        in_specs=[row(D_MODEL), row(GROUP_W), row(GROUP_W), row(GROUP_W), const(w_out3), const(g),
                  const(w_hi), const(w_lo), const(b_rt)],
        out_specs=[row(D_MODEL), row(LANES), row(LANES)],
        out_shape=[jax.ShapeDtypeStruct((S, D_MODEL), F32), jax.ShapeDtypeStruct((S, LANES), jnp.int32),
                   jax.ShapeDtypeStruct((S, LANES), F32)],
        compiler_params=_params("arbitrary"),
        name="out_router",
    )(x, o_ret, o_sb, o_fx, w_out3, g, w_hi, w_lo, b_rt)


def _row_gather(src_hbm, dst, sem, idx_ref, base, n):
    def start():
        def one(r, c):
            pltpu.make_async_copy(src_hbm.at[pl.ds(idx_ref[base + r], 1), :], dst.at[pl.ds(r, 1), :], sem).start()
            return c

        lax.fori_loop(0, n, one, 0, unroll=8)

    def wait():
        pltpu.make_async_copy(src_hbm.at[pl.ds(0, n), :], dst, sem).wait()

    return start, wait


def _expert_kernel(be_ref, tok_ref, nused_ref, x_hbm, g_ref, wg_ref, wu_ref, wd_ref, y_ref, xbuf, sem, *, rb):
    b = pl.program_id(0)
    nused = nused_ref[0]
    slot = b % 2

    def gather(blk, s):
        return _row_gather(x_hbm, xbuf.at[s], sem.at[s], tok_ref, blk * rb, rb)

    @pl.when(b == 0)
    def _():
        gather(0, 0)[0]()

    @pl.when(b + 1 < nused)
    def _():
        gather(b + 1, 1 - slot)[0]()

    @pl.when(b < nused)
    def _():
        gather(b, slot)[1]()
        h = _rmsnorm(xbuf[slot], g_ref[...]).astype(BF16)
        gt = _dot(h, wg_ref[0].astype(BF16))
        up = _dot(h, wu_ref[0].astype(BF16))
        hid = gt * (1.0 / (1.0 + jnp.exp(-gt))) * up
        y_ref[...] = _dot(hid.astype(BF16), wd_ref[0].astype(BF16))

    @pl.when(b >= nused)
    def _():
        y_ref[...] = jnp.zeros_like(y_ref)


def _experts(blk_expert, tok_slot, nused, x1, g, w_gate, w_up, w_down, rb):
    P = tok_slot.shape[0]
    nblk = P // rb
    live = lambda b, nu: jnp.minimum(b, nu[0] - 1)
    grid_spec = pltpu.PrefetchScalarGridSpec(
        num_scalar_prefetch=3,
        grid=(nblk,),
        in_specs=[pl.BlockSpec(memory_space=pl.ANY),
                  pl.BlockSpec((1, D_MODEL), lambda b, be, tk, nu: (0, 0)),
                  pl.BlockSpec((1, D_MODEL, D_EXPERT), lambda b, be, tk, nu: (be[live(b, nu)], 0, 0)),
                  pl.BlockSpec((1, D_MODEL, D_EXPERT), lambda b, be, tk, nu: (be[live(b, nu)], 0, 0)),
                  pl.BlockSpec((1, D_EXPERT, D_MODEL), lambda b, be, tk, nu: (be[live(b, nu)], 0, 0))],
        out_specs=pl.BlockSpec((rb, D_MODEL), lambda b, be, tk, nu: (b, 0)),
        scratch_shapes=[pltpu.VMEM((2, rb, D_MODEL), F32), pltpu.SemaphoreType.DMA((2,))],
    )
    return pl.pallas_call(
        functools.partial(_expert_kernel, rb=rb),
        grid_spec=grid_spec,
        out_shape=jax.ShapeDtypeStruct((P, D_MODEL), F32),
        compiler_params=_params("arbitrary"),
        name="experts",
    )(blk_expert, tok_slot, nused, x1, g, w_gate, w_up, w_down)


def _combine_kernel(dest_ref, x_ref, gate_ref, y_hbm, gf_ref, o_ref, ybuf, sem, *, tb, final):
    b = pl.program_id(0)
    nb = pl.num_programs(0)
    slot = b % 2

    def gather(blk, s):
        return _row_gather(y_hbm, ybuf.at[s], sem.at[s], dest_ref, blk * (TOP_K * tb), TOP_K * tb)

    @pl.when(b == 0)
    def _():
        gather(0, 0)[0]()

    @pl.when(b + 1 < nb)
    def _():
        gather(b + 1, 1 - slot)[0]()

    gather(b, slot)[1]()
    gate = gate_ref[...]
    y = ybuf[slot]
    out = x_ref[...] + (gate[:, 0:1] * y[:tb] + gate[:, 1:2] * y[tb:])
    if final:
        out = _rmsnorm(out, gf_ref[...])
    o_ref[...] = out


def _combine(dest_blocked, x1, gates, ybuf, g_final, tb, final):
    S = x1.shape[0]
    grid_spec = pltpu.PrefetchScalarGridSpec(
        num_scalar_prefetch=1,
        grid=(S // tb,),
        in_specs=[pl.BlockSpec((tb, D_MODEL), lambda b, d: (b, 0)),
                  pl.BlockSpec((tb, LANES), lambda b, d: (b, 0)),
                  pl.BlockSpec(memory_space=pl.ANY),
                  pl.BlockSpec((1, D_MODEL), lambda b, d: (0, 0))],
        out_specs=pl.BlockSpec((tb, D_MODEL), lambda b, d: (b, 0)),
        scratch_shapes=[pltpu.VMEM((2, TOP_K * tb, D_MODEL), F32), pltpu.SemaphoreType.DMA((2,))],
    )
    return pl.pallas_call(
        functools.partial(_combine_kernel, tb=tb, final=final),
        grid_spec=grid_spec,
        out_shape=jax.ShapeDtypeStruct((S, D_MODEL), F32),
        compiler_params=_params("arbitrary"),
        name="combine",
    )(dest_blocked, x1, gates, ybuf, g_final)


def _pad_cols(w, width):
    return jnp.pad(w, ((0, 0), (0, width - w.shape[1])))


def _swap_cols(w, n_heads):
    w4 = w.reshape(w.shape[0], n_heads, 2, HALF)
    return jnp.stack([-w4[:, :, 1], w4[:, :, 0]], axis=2).reshape(w.shape)


def _proj_weights(w_in):
    w_ret, w_sb, w_fox = N_RET * HEAD_DIM, N_SB * HEAD_DIM, N_FOX * HEAD_DIM
    sizes = [w_ret] * 4 + [w_sb] * 3 + [w_fox] * 3 + [N_FOX]
    offs = np.concatenate([[0], np.cumsum(sizes)])
    q_r, k_r, v_r, gate_r, q_s, k_s, v_s, q_f, k_f, v_f, f_l = [w_in[:, offs[n]:offs[n + 1]] for n in range(11)]
    q_r = q_r * Q_SCALE
    segs = [q_r, _swap_cols(q_r, N_RET), k_r, _swap_cols(k_r, N_RET), v_r, gate_r,
            q_s * Q_SCALE, k_s, v_s, q_f * Q_SCALE, k_f, v_f]
    cols = [_pad_cols(s, GROUP_W) for s in segs] + [_pad_cols(f_l, LANES)]
    return jnp.concatenate(cols, axis=1).astype(BF16)


def _rope_tables(S):
    inv = 1.0 / (ROPE_BASE ** (jnp.arange(HALF, dtype=F32) / HALF))
    ang = jnp.arange(S).astype(F32)[:, None] * inv[None, :]
    cos, sin = jnp.cos(ang), jnp.sin(ang)
    return jnp.concatenate([cos] * 4, axis=1), jnp.concatenate([sin] * 4, axis=1)


def _routing_tables(idx, rb):
    T = idx.shape[0]
    A = T * TOP_K
    P = A + N_EXPERTS * rb
    e = idx[:, :TOP_K].reshape(A)
    onehot = (e[:, None] == jnp.arange(N_EXPERTS, dtype=jnp.int32)[None, :]).astype(jnp.int32)
    cs = jnp.cumsum(onehot, axis=0)
    rank = jnp.sum(cs * onehot, axis=1) - 1
    counts = cs[-1]
    padded = ((counts + rb - 1) // rb) * rb
    pend = jnp.cumsum(padded)
    pstart = pend - padded
    dest = (jnp.sum(onehot * pstart[None, :], axis=1) + rank).astype(jnp.int32)
    tok_slot = jnp.zeros((P,), jnp.int32).at[dest].set(jnp.arange(A, dtype=jnp.int32) // TOP_K)
    blk_start = jnp.arange(P // rb, dtype=jnp.int32) * rb
    blk_expert = jnp.minimum(jnp.sum((pend[None, :] <= blk_start[:, None]).astype(jnp.int32), axis=1),
                             N_EXPERTS - 1).astype(jnp.int32)
    nused = (pend[-1] // rb).astype(jnp.int32).reshape(1)
    return dest, tok_slot, blk_expert, nused


def kernel(x, norm_mix, w_in, b_forget, g_ret, g_sb, g_fox, w_out, norm_ffn, w_group, b_group,
           w_router, b_router, w_gate, w_up, w_down, norm_final):
    S = x.shape[1]
    ts = min(512, S)
    tq = min(256, S)
    rb = 256
    tb = min(256, S)
    depth = w_in.shape[0]
    xs = x.reshape(S, D_MODEL)
    cos_t, sin_t = _rope_tables(S)
    row1 = lambda v, width: jnp.pad(v, (0, width - v.shape[0])).reshape(1, width)
    nq = S // tq

    for l in range(depth):
        w_all = _proj_weights(w_in[l])
        qr, kr, vr, gate, qs, ks, vs, qf, kf, vf, fcum = _proj(
            xs, norm_mix[l].reshape(1, D_MODEL), w_all, cos_t, sin_t, row1(b_forget[l], LANES), ts)
        o_ret = _retention(qr, kr, vr, gate, row1(g_ret[l], GROUP_W), tq)
        o_sb = _stick_breaking(qs, ks, vs, row1(g_sb[l], GROUP_W), tq)
        f_pairs = fcum[:, :2 * PAIRS].T.reshape(PAIRS, 2, nq, tq).transpose(0, 2, 1, 3)
        o_fx = _fox(qf, kf, vf, f_pairs, row1(g_fox[l], GROUP_W), tq)

        w_o = w_out[l]
        w_ret, w_sb = N_RET * HEAD_DIM, N_SB * HEAD_DIM
        pad_rows = lambda w: jnp.pad(w, ((0, GROUP_W - w.shape[0]), (0, 0)))
        w_out3 = jnp.stack([pad_rows(w_o[:w_ret]), pad_rows(w_o[w_ret:w_ret + w_sb]),
                            pad_rows(w_o[w_ret + w_sb:])]).astype(BF16)
        w_rt = _pad_cols(jnp.concatenate([w_group[l], w_router[l]], axis=1), LANES)
        w_hi = w_rt.astype(BF16)
        w_lo = (w_rt - w_hi.astype(F32)).astype(BF16)
        b_rt = row1(jnp.concatenate([b_group[l], b_router[l]]), LANES)
        g_ffn = norm_ffn[l].reshape(1, D_MODEL)
        x1, ridx, rgate = _out_router(xs, o_ret, o_sb, o_fx, w_out3, g_ffn, w_hi, w_lo, b_rt, ts)

        dest, tok_slot, blk_expert, nused = _routing_tables(ridx, rb)
        ybuf = _experts(blk_expert, tok_slot, nused, x1, g_ffn, w_gate[l], w_up[l], w_down[l], rb)
        dest_blocked = dest.reshape(S // tb, tb, TOP_K).transpose(0, 2, 1).reshape(-1)
        xs = _combine(dest_blocked, x1, rgate, ybuf, norm_final.reshape(1, D_MODEL), tb, l == depth - 1)
    return xs.reshape(x.shape)
```

```python
import functools

import jax
import jax.numpy as jnp
import numpy as np
from jax import lax
from jax.experimental import pallas as pl
from jax.experimental.pallas import tpu as pltpu

F32 = jnp.float32
BF16 = jnp.bfloat16

D_MODEL = 1024
HEAD_DIM = 64
HALF = HEAD_DIM // 2
N_RET, N_SB, N_FOX = 6, 5, 5
PAIRS = 3
GROUP_W = PAIRS * 128
CHUNK = 64
ROPE_BASE = 10000.0
EPS = 1e-6
N_GROUPS, EXPERTS_PER_GROUP = 4, 8
N_EXPERTS = N_GROUPS * EXPERTS_PER_GROUP
TOP_K = 2
D_EXPERT = 512
Q_SCALE = HEAD_DIM ** -0.5

LOG2E = 1.4426950408889634
EXP2_ZERO_BELOW = -159.0
BOUND_SLACK = 1.0 + 2.0 ** -10

LANES = 128
VMEM_LIMIT = 56 * 1024 * 1024

SEG_QR, SEG_QR_SW, SEG_KR, SEG_KR_SW, SEG_VR, SEG_GATE = 0, 1, 2, 3, 4, 5
SEG_QS, SEG_KS, SEG_VS, SEG_QF, SEG_KF, SEG_VF = 6, 7, 8, 9, 10, 11
N_SEG = 12
PROJ_W = N_SEG * GROUP_W + LANES


def _params(*sem):
    return pltpu.CompilerParams(dimension_semantics=sem, vmem_limit_bytes=VMEM_LIMIT)


def _rmsnorm(x, g):
    return x * lax.rsqrt(jnp.mean(x * x, axis=-1, keepdims=True) + EPS) * g


def _split3(x):
    hi = x.astype(BF16)
    r = x - hi.astype(F32)
    mid = r.astype(BF16)
    lo = (r - mid.astype(F32)).astype(BF16)
    return hi, mid, lo


def _dot(a, b):
    return jnp.dot(a, b, preferred_element_type=F32)


def _dot_nt(a, b):
    return lax.dot_general(a, b, (((1,), (1,)), ((), ())), preferred_element_type=F32)


def _proj_kernel(x_ref, g_ref, w_ref, cos_ref, sin_ref, bf_ref, tri_ref,
                 qr_ref, kr_ref, vr_ref, gate_ref, qs_ref, ks_ref, vs_ref, qf_ref, kf_ref, vf_ref,
                 fcum_ref, carry_ref):
    @pl.when(pl.program_id(0) == 0)
    def _():
        carry_ref[...] = jnp.zeros_like(carry_ref)

    h = _rmsnorm(x_ref[...], g_ref[...]).astype(BF16)

    def seg(k, width=GROUP_W):
        return _dot(h, w_ref[:, k * GROUP_W:k * GROUP_W + width])

    cos = jnp.concatenate([cos_ref[...]] * PAIRS, axis=1)
    sin = jnp.concatenate([sin_ref[...]] * PAIRS, axis=1)
    qr_ref[...] = (seg(SEG_QR) * cos + seg(SEG_QR_SW) * sin).astype(BF16)
    kr_ref[...] = (seg(SEG_KR) * cos + seg(SEG_KR_SW) * sin).astype(BF16)
    vr_ref[...] = seg(SEG_VR).astype(BF16)
    gate = seg(SEG_GATE)
    gate_ref[...] = gate * (1.0 / (1.0 + jnp.exp(-gate)))
    qs_ref[...] = seg(SEG_QS).astype(BF16)
    ks_ref[...] = seg(SEG_KS).astype(BF16)
    vs_ref[...] = seg(SEG_VS).astype(BF16)
    qf_ref[...] = seg(SEG_QF).astype(BF16)
    kf_ref[...] = seg(SEG_KF).astype(BF16)
    vf_ref[...] = seg(SEG_VF).astype(BF16)

    fl = seg(N_SEG, LANES) + bf_ref[...]
    logf = jnp.minimum(fl, 0.0) - jnp.log1p(jnp.exp(-jnp.abs(fl)))
    hi, mid, lo = _split3(logf)
    tri = tri_ref[...]
    fc = _dot(tri, hi) + _dot(tri, mid) + _dot(tri, lo) + carry_ref[...]
    fcum_ref[...] = fc
    carry_ref[...] = fc[-1:, :]


def _proj(x, g, w_all, cos_t, sin_t, bf_pad, ts):
    S = x.shape[0]
    tri = jnp.tril(jnp.ones((ts, ts), F32)).astype(BF16)
    row = lambda w: pl.BlockSpec((ts, w), lambda i: (i, 0))
    const = lambda a: pl.BlockSpec(a.shape, lambda i: (0,) * a.ndim)
    bshape = jax.ShapeDtypeStruct((S, GROUP_W), BF16)
    outs = [bshape, bshape, bshape, jax.ShapeDtypeStruct((S, GROUP_W), F32)] + [bshape] * 6 + [
        jax.ShapeDtypeStruct((S, LANES), F32)]
    return pl.pallas_call(
        _proj_kernel,
        grid=(S // ts,),
        in_specs=[row(D_MODEL), const(g), const(w_all), row(LANES), row(LANES), const(bf_pad), const(tri)],
        out_specs=[row(GROUP_W)] * 10 + [row(LANES)],
        out_shape=outs,
        scratch_shapes=[pltpu.VMEM((1, LANES), F32)],
        compiler_params=_params("arbitrary"),
        name="proj",
    )(x, g, w_all, cos_t, sin_t, bf_pad, tri)


def _half_masks(shape):
    lane = lax.broadcasted_iota(jnp.int32, shape, len(shape) - 1)
    return lane < HEAD_DIM, lane >= HEAD_DIM


def _per_head_mean(x, m0):
    s0 = jnp.sum(jnp.where(m0, x, 0.0), axis=1, keepdims=True)
    s1 = jnp.sum(jnp.where(m0, 0.0, x), axis=1, keepdims=True)
    return jnp.where(m0, s0, s1) * (1.0 / HEAD_DIM)


def _ret_kernel(q_ref, k_ref, v_ref, gate_ref, dmask_ref, kdec_ref, qdec_ref, cdec_ref, g_ref,
                o_ref, state_ref):
    @pl.when(pl.program_id(1) == 0)
    def _():
        state_ref[...] = jnp.zeros_like(state_ref)

    q, k, v = q_ref[...], k_ref[...], v_ref[...]
    m0, m1 = _half_masks(q.shape)
    state = state_ref[...]
    qd = (q.astype(F32) * qdec_ref[0]).astype(BF16)
    o = _dot(qd, state.astype(BF16))
    for hh, m in enumerate((m0, m1)):
        s = _dot_nt(jnp.where(m, q, jnp.zeros_like(q)), k)
        oh = _dot((s * dmask_ref[hh]).astype(BF16), v)
        o = o + jnp.where(m, oh, 0.0)
    kdt = (k.astype(F32) * kdec_ref[0]).T.astype(BF16)
    u = _dot(kdt, v)
    r = lax.broadcasted_iota(jnp.int32, u.shape, 0) < HEAD_DIM
    c = lax.broadcasted_iota(jnp.int32, u.shape, 1) < HEAD_DIM
    state_ref[...] = state * cdec_ref[0] + jnp.where(r == c, u, 0.0)
    cen = o - _per_head_mean(o, m0)
    var = _per_head_mean(cen * cen, m0)
    o_ref[...] = (cen * lax.rsqrt(var + EPS) * g_ref[...] * gate_ref[...]).astype(BF16)


def _ret_tables(B):
    hs = np.arange(N_RET, dtype=np.float64)
    lg = np.log(1.0 - 2.0 ** (-5.0 - hs))
    idx = np.arange(B, dtype=np.float64)
    diff = idx[:, None] - idx[None, :]
    ch = np.arange(B) // CHUNK
    same = ch[:, None] == ch[None, :]
    earlier = ch[None, :] < ch[:, None]
    expo = np.where(same, np.abs(diff), diff)
    dmask = np.where(same | earlier, np.exp(lg[:, None, None] * expo[None]), 0.0)
    kdec = np.exp(lg[:, None] * (B - 1.0 - idx)[None])
    qdec = np.exp(lg[:, None] * (idx + 1.0)[None])
    cdec = np.exp(lg * B)
    lane_head = np.arange(LANES) // HEAD_DIM
    kdec_p = np.stack([kdec[2 * p + lane_head].T for p in range(PAIRS)])
    qdec_p = np.stack([qdec[2 * p + lane_head].T for p in range(PAIRS)])
    cdec_p = np.stack([np.broadcast_to(cdec[2 * p + lane_head][None, :], (LANES, LANES)) for p in range(PAIRS)])
    f = lambda a: jnp.asarray(a, F32)
    return f(dmask), f(kdec_p), f(qdec_p), f(cdec_p)


def _retention(qr, kr, vr, gate, g_pad, B):
    S = qr.shape[0]
    dmask, kdec, qdec, cdec = _ret_tables(B)
    blk = pl.BlockSpec((B, LANES), lambda p, b: (b, p))
    return pl.pallas_call(
        _ret_kernel,
        grid=(PAIRS, S // B),
        in_specs=[blk, blk, blk, blk,
                  pl.BlockSpec((2, B, B), lambda p, b: (p, 0, 0)),
                  pl.BlockSpec((1, B, LANES), lambda p, b: (p, 0, 0)),
                  pl.BlockSpec((1, B, LANES), lambda p, b: (p, 0, 0)),
                  pl.BlockSpec((1, LANES, LANES), lambda p, b: (p, 0, 0)),
                  pl.BlockSpec((1, LANES), lambda p, b: (0, p))],
        out_specs=blk,
        out_shape=jax.ShapeDtypeStruct((S, GROUP_W), BF16),
        scratch_shapes=[pltpu.VMEM((LANES, LANES), F32)],
        compiler_params=_params("arbitrary", "arbitrary"),
        name="retention",
    )(qr, kr, vr, gate, dmask, kdec, qdec, cdec, g_pad)


def _head_rms_out(acc0, acc1, g, m0):
    o = jnp.where(m0, acc0, acc1)
    ms = _per_head_mean(o * o, m0)
    return (o * lax.rsqrt(ms + EPS) * g).astype(BF16)


def _pair_rows(q, m0):
    zero = jnp.zeros_like(q)
    return jnp.concatenate([jnp.where(m0, q, zero), jnp.where(m0, zero, q)], axis=0)


def _rows2(a0, a1, tq):
    w = a0.shape[1]
    return jnp.concatenate([jnp.broadcast_to(a0, (tq, w)), jnp.broadcast_to(a1, (tq, w))], axis=0)


def _max_key_norm(k_ref, kmax_ref, tq, m0):
    def blk(j, mx):
        kk = k_ref[pl.ds(pl.multiple_of(j * tq, tq), tq), :].astype(F32)
        n2 = _per_head_mean(kk * kk, m0) * float(HEAD_DIM)
        return jnp.maximum(mx, jnp.max(n2, axis=0, keepdims=True))

    n2max = lax.fori_loop(0, k_ref.shape[0] // tq, blk, jnp.zeros((1, LANES), F32))
    kmax_ref[...] = jnp.sqrt(n2max)


def _score_bound(qm, kmax_ref, tq):
    qf = qm.astype(F32)
    qn = jnp.sqrt(jnp.sum(qf * qf, axis=1, keepdims=True))
    kn = _rows2(kmax_ref[:, 0:1], kmax_ref[:, HEAD_DIM:HEAD_DIM + 1], tq)
    return qn * kn * BOUND_SLACK


def _sb_kernel(q_ref, k_ref, v_ref, tri_ref, g_ref, o_ref, acc_ref, kmax_ref, *, tq):
    i = pl.program_id(1)
    m0, _ = _half_masks((tq, LANES))

    @pl.when(i == 0)
    def _():
        _max_key_norm(k_ref, kmax_ref, tq, m0)

    qm = _pair_rows(q_ref[...], m0)
    zbound = _score_bound(qm, kmax_ref, tq)
    row = lax.broadcasted_iota(jnp.int32, (2 * tq, tq), 0) & (tq - 1)
    col = lax.broadcasted_iota(jnp.int32, (2 * tq, tq), 1)
    acc_ref[...] = jnp.zeros_like(acc_ref)

    def step(j, carry, diag):
        ks = k_ref[pl.ds(pl.multiple_of(j * tq, tq), tq), :]
        vs = v_ref[pl.ds(pl.multiple_of(j * tq, tq), tq), :]
        tri = tri_ref[...]
        z = _dot_nt(qm, ks)
        l1m = -(jnp.maximum(z, 0.0) + jnp.log2(1.0 + jnp.exp2(-jnp.abs(z))))
        if diag:
            l1m = jnp.where(col < row, l1m, 0.0)
        hi = l1m.astype(BF16)
        lo = (l1m - hi.astype(F32)).astype(BF16)
        csum = _dot(hi, tri) + _dot(lo, tri) + carry
        a = jnp.exp2(z + csum)
        if diag:
            a = jnp.where(col < row, a, 0.0)
        acc_ref[...] += _dot(a.astype(BF16), vs)
        return csum[:, 0:1]

    def alive(carry):
        return jnp.max(carry + zbound) >= EXP2_ZERO_BELOW

    def body(st):
        j, carry, _ = st
        carry = step(j, carry, False)
        return j - 1, carry, alive(carry)

    carry = step(i, jnp.zeros((2 * tq, 1), F32), True)
    lax.while_loop(lambda st: jnp.logical_and(st[0] >= 0, st[2]), body, (i - 1, carry, alive(carry)))
    o_ref[...] = _head_rms_out(acc_ref[:tq], acc_ref[tq:], g_ref[...], m0)


def _stick_breaking(q, k, v, g_pad, tq):
    S = q.shape[0]
    tri = jnp.tril(jnp.ones((tq, tq), F32)).astype(BF16)
    return pl.pallas_call(
        functools.partial(_sb_kernel, tq=tq),
        grid=(PAIRS, S // tq),
        in_specs=[pl.BlockSpec((tq, LANES), lambda p, i: (i, p)),
                  pl.BlockSpec((S, LANES), lambda p, i: (0, p)),
                  pl.BlockSpec((S, LANES), lambda p, i: (0, p)),
                  pl.BlockSpec((tq, tq), lambda p, i: (0, 0)),
                  pl.BlockSpec((1, LANES), lambda p, i: (0, p))],
        out_specs=pl.BlockSpec((tq, LANES), lambda p, i: (i, p)),
        out_shape=jax.ShapeDtypeStruct((S, GROUP_W), BF16),
        scratch_shapes=[pltpu.VMEM((2 * tq, LANES), F32), pltpu.VMEM((1, LANES), F32)],
        compiler_params=_params("arbitrary", "arbitrary"),
        name="stick_breaking",
    )(q, k, v, tri, g_pad)


def _fox_kernel(q_ref, k_ref, v_ref, f_ref, g_ref, o_ref, acc_ref, kmax_ref, s_ref, *, tq):
    i = pl.program_id(1)
    m0, _ = _half_masks((tq, LANES))

    @pl.when(i == 0)
    def _():
        _max_key_norm(k_ref, kmax_ref, tq, m0)

    qm = _pair_rows(q_ref[...], m0)
    sbound = _score_bound(qm, kmax_ref, tq)
    row = lax.broadcasted_iota(jnp.int32, (2 * tq, tq), 0) & (tq - 1)
    col = lax.broadcasted_iota(jnp.int32, (2 * tq, tq), 1)
    acc_ref[...] = jnp.zeros_like(acc_ref)

    fref = [f_ref[0, i, h:h + 1, 0:1] for h in (0, 1)]
    fref_rows = _rows2(fref[0], fref[1], tq)

    def kblock(ref, j):
        return ref[pl.ds(pl.multiple_of(jnp.maximum(j, 0) * tq, tq), tq), :]

    def scores(j, slot):
        s_ref[slot] = _dot_nt(qm, kblock(k_ref, j))

    def step(j, slot, m_run, diag):
        scores(j - 1, 1 - slot)
        jj = jnp.maximum(j, 0)
        vs = kblock(v_ref, j)
        one = jnp.ones_like(vs)
        vaug = (jnp.where(m0, vs, one), jnp.where(m0, one, vs))
        bias = [jnp.where(j >= 0, fref[h] - f_ref[0, jj, h:h + 1, :], -jnp.inf) for h in (0, 1)]
        t = s_ref[slot] + _rows2(bias[0], bias[1], tq)
        if diag:
            t = jnp.where(col <= row, t, -jnp.inf)
        m_new = jnp.maximum(m_run, jnp.max(t, axis=1, keepdims=True))
        pr = jnp.exp2(t - m_new).astype(BF16)
        pv = jnp.concatenate([_dot(pr[:tq], vaug[0]), _dot(pr[tq:], vaug[1])], axis=0)
        acc_ref[...] = jnp.exp2(m_run - m_new) * acc_ref[...] + pv
        return m_new

    def alive(j, m_run):
        jj = jnp.maximum(j, 0)
        fend = _rows2(f_ref[0, jj, 0:1, tq - 1:tq], f_ref[0, jj, 1:2, tq - 1:tq], tq)
        return jnp.max(sbound - m_run + (fref_rows - fend)) >= EXP2_ZERO_BELOW

    def body(st):
        j, m_run, _ = st
        m_run = step(j, 0, m_run, False)
        m_run = step(j - 1, 1, m_run, False)
        return j - 2, m_run, alive(j - 2, m_run)

    scores(i, 1)
    m_run = step(i, 1, jnp.full((2 * tq, 1), -jnp.inf, F32), True)
    lax.while_loop(lambda st: jnp.logical_and(st[0] >= 0, st[2]), body, (i - 1, m_run, alive(i - 1, m_run)))
    acc0, acc1 = acc_ref[:tq], acc_ref[tq:]
    acc0 = acc0 / jnp.max(jnp.where(m0, -jnp.inf, acc0), axis=1, keepdims=True)
    acc1 = acc1 / jnp.max(jnp.where(m0, acc1, -jnp.inf), axis=1, keepdims=True)
    o_ref[...] = _head_rms_out(acc0, acc1, g_ref[...], m0)


def _fox(q, k, v, f_pairs, g_pad, tq):
    S = q.shape[0]
    nq = S // tq
    return pl.pallas_call(
        functools.partial(_fox_kernel, tq=tq),
        grid=(PAIRS, nq),
        in_specs=[pl.BlockSpec((tq, LANES), lambda p, i: (i, p)),
                  pl.BlockSpec((S, LANES), lambda p, i: (0, p)),
                  pl.BlockSpec((S, LANES), lambda p, i: (0, p)),
                  pl.BlockSpec((1, nq, 2, tq), lambda p, i: (p, 0, 0, 0)),
                  pl.BlockSpec((1, LANES), lambda p, i: (0, p))],
        out_specs=pl.BlockSpec((tq, LANES), lambda p, i: (i, p)),
        out_shape=jax.ShapeDtypeStruct((S, GROUP_W), BF16),
        scratch_shapes=[pltpu.VMEM((2 * tq, LANES), F32), pltpu.VMEM((1, LANES), F32),
                        pltpu.VMEM((2, 2 * tq, tq), F32)],
        compiler_params=_params("arbitrary", "arbitrary"),
        name="fox",
    )(q, k, v, f_pairs, g_pad)


def _out_router_kernel(x_ref, ret_ref, sb_ref, fx_ref, w_ref, g_ref, whi_ref, wlo_ref, b_ref,
                       x1_ref, x1t_ref, idx_ref, gate_ref):
    mix = _dot(ret_ref[...], w_ref[0]) + _dot(sb_ref[...], w_ref[1]) + _dot(fx_ref[...], w_ref[2])
    x1 = x_ref[...] + mix
    x1_ref[...] = x1
    _store_row_tiles(x1t_ref, x1, x1.shape[0])
    h = _rmsnorm(x1, g_ref[...])
    hi = h.astype(BF16)
    lo = (h - hi.astype(F32)).astype(BF16)
    whi = whi_ref[...]
    logits = _dot(hi, whi) + _dot(lo, whi) + _dot(hi, wlo_ref[...]) + b_ref[...]
    lane = lax.broadcasted_iota(jnp.int32, logits.shape, 1).astype(F32)
    neg = -jnp.inf
    big = float(LANES)

    def top(vals):
        v = jnp.max(vals, axis=1, keepdims=True)
        at = jnp.min(jnp.where(vals == v, lane, big), axis=1, keepdims=True)
        return v, at

    lg = jnp.where(lane < N_GROUPS, logits, neg)
    gmax, grp = top(lg)
    p_grp = 1.0 / jnp.sum(jnp.exp(lg - gmax), axis=1, keepdims=True)
    lo_lane = N_GROUPS + EXPERTS_PER_GROUP * grp
    le = jnp.where((lane >= lo_lane) & (lane < lo_lane + EXPERTS_PER_GROUP), logits, neg)
    v1, i1 = top(le)
    v2, i2 = top(jnp.where(lane == i1, neg, le))
    e21 = jnp.exp(v2 - v1)
    g1 = p_grp / (1.0 + e21)
    g2 = p_grp * e21 / (1.0 + e21)
    idx = jnp.where(lane == 0.0, i1, i2) - float(N_GROUPS)
    idx_ref[...] = jnp.where(lane < TOP_K, idx, 0.0).astype(jnp.int32)
    gate_ref[...] = jnp.where(lane == 0.0, g1, jnp.where(lane == 1.0, g2, 0.0))


def _out_router(x, o_ret, o_sb, o_fx, w_out3, g, w_hi, w_lo, b_rt, ts):
    S = x.shape[0]
    row = lambda w: pl.BlockSpec((ts, w), lambda i: (i, 0))
    const = lambda a: pl.BlockSpec(a.shape, lambda i: (0,) * a.ndim)
    return pl.pallas_call(
        _out_router_kernel,
        grid=(S // ts,),
        in_specs=[row(D_MODEL), row(GROUP_W), row(GROUP_W), row(GROUP_W), const(w_out3), const(g),
                  const(w_hi), const(w_lo), const(b_rt)],
        out_specs=[row(D_MODEL), pl.BlockSpec((ts * ROW_TILE, LANES), lambda i: (i, 0)), row(LANES), row(LANES)],
        out_shape=[jax.ShapeDtypeStruct((S, D_MODEL), F32), jax.ShapeDtypeStruct((S * ROW_TILE, LANES), F32),
                   jax.ShapeDtypeStruct((S, LANES), jnp.int32), jax.ShapeDtypeStruct((S, LANES), F32)],
        compiler_params=_params("arbitrary"),
        name="out_router",
    )(x, o_ret, o_sb, o_fx, w_out3, g, w_hi, w_lo, b_rt)


ROW_TILE = D_MODEL // LANES


def _store_row_tiles(ref, val, n):
    for c in range(ROW_TILE):
        ref[pl.ds(c, n, stride=ROW_TILE), :] = val[:, c * LANES:(c + 1) * LANES]


def _load_row_tiles(ref, first, n):
    return jnp.concatenate([ref[pl.ds(first * ROW_TILE + c, n, stride=ROW_TILE), :] for c in range(ROW_TILE)], axis=1)


def _tile_gather(src_hbm, dst, sem, idx_ref, base, n):
    def one(r):
        src_row = pl.multiple_of(idx_ref[base + r] * ROW_TILE, ROW_TILE)
        pltpu.make_async_copy(src_hbm.at[pl.ds(src_row, ROW_TILE), :], dst.at[pl.ds(r * ROW_TILE, ROW_TILE), :],
                              sem).start()

    def start_loop():
        lax.fori_loop(0, n, lambda r, c: (one(r), c)[1], 0, unroll=8)

    def start_unrolled():
        for r in range(n):
            one(r)

    def wait():
        pltpu.make_async_copy(src_hbm.at[pl.ds(0, n * ROW_TILE), :], dst, sem).wait()

    return start_loop, start_unrolled, wait


def _expert_kernel(be_ref, tok_ref, nused_ref, x_hbm, g_ref, wg_ref, wu_ref, wd_ref, y_ref,
                   xbuf, wg_bf, wu_bf, wd_bf, sem, *, rb):
    b = pl.program_id(0)
    nused = nused_ref[0]
    slot = b % 2

    def gather(blk, s):
        return _tile_gather(x_hbm, xbuf.at[s], sem.at[s], tok_ref, blk * rb, rb)

    @pl.when(b == 0)
    def _():
        gather(0, 0)[0]()

    @pl.when(b < nused)
    def _():
        @pl.when(jnp.logical_or(b == 0, be_ref[b] != be_ref[jnp.maximum(b - 1, 0)]))
        def _():
            wg_bf[...] = wg_ref[0].astype(BF16)
            wu_bf[...] = wu_ref[0].astype(BF16)
            wd_bf[...] = wd_ref[0].astype(BF16)

        gather(b, slot)[2]()
        gather(jnp.minimum(b + 1, nused - 1), 1 - slot)[1]()
        h = _rmsnorm(_load_row_tiles(xbuf.at[slot], 0, rb), g_ref[...]).astype(BF16)
        gt = _dot(h, wg_bf[...])
        up = _dot(h, wu_bf[...])
        hid = gt * (1.0 / (1.0 + jnp.exp(-gt))) * up
        _store_row_tiles(y_ref, _dot(hid.astype(BF16), wd_bf[...]), rb)

    @pl.when(b == nused)
    def _():
        gather(b, slot)[2]()

    @pl.when(b >= nused)
    def _():
        y_ref[...] = jnp.zeros_like(y_ref)


def _experts(blk_expert, tok_slot, nused, x1, g, w_gate, w_up, w_down, layer, rb):
    P = tok_slot.shape[0]
    nblk = P // rb
    live = lambda b, nu: jnp.minimum(b, nu[0] - 1)
    wsel = lambda b, be, tk, nu: (layer * N_EXPERTS + be[live(b, nu)], 0, 0)
    grid_spec = pltpu.PrefetchScalarGridSpec(
        num_scalar_prefetch=3,
        grid=(nblk,),
        in_specs=[pl.BlockSpec(memory_space=pl.ANY),
                  pl.BlockSpec((1, D_MODEL), lambda b, be, tk, nu: (0, 0)),
                  pl.BlockSpec((1, D_MODEL, D_EXPERT), wsel),
                  pl.BlockSpec((1, D_MODEL, D_EXPERT), wsel),
                  pl.BlockSpec((1, D_EXPERT, D_MODEL), wsel)],
        out_specs=pl.BlockSpec((rb * ROW_TILE, LANES), lambda b, be, tk, nu: (b, 0)),
        scratch_shapes=[pltpu.VMEM((2, rb * ROW_TILE, LANES), F32),
                        pltpu.VMEM((D_MODEL, D_EXPERT), BF16), pltpu.VMEM((D_MODEL, D_EXPERT), BF16),
                        pltpu.VMEM((D_EXPERT, D_MODEL), BF16), pltpu.SemaphoreType.DMA((2,))],
    )
    return pl.pallas_call(
        functools.partial(_expert_kernel, rb=rb),
        grid_spec=grid_spec,
        out_shape=jax.ShapeDtypeStruct((P * ROW_TILE, LANES), F32),
        compiler_params=_params("arbitrary"),
        name="experts",
    )(blk_expert, tok_slot, nused, x1, g, w_gate, w_up, w_down)


def _combine_kernel(dest_ref, x_ref, gate_ref, y_hbm, gf_ref, o_ref, ybuf, sem, *, tb, final):
    b = pl.program_id(0)
    nb = pl.num_programs(0)
    slot = b % 2

    def gather(blk, s):
        return _tile_gather(y_hbm, ybuf.at[s], sem.at[s], dest_ref, blk * (TOP_K * tb), TOP_K * tb)

    @pl.when(b == 0)
    def _():
        gather(0, 0)[0]()

    @pl.when(b + 1 < nb)
    def _():
        gather(b + 1, 1 - slot)[0]()

    gather(b, slot)[2]()
    gate = gate_ref[...]
    y0 = _load_row_tiles(ybuf.at[slot], 0, tb)
    y1 = _load_row_tiles(ybuf.at[slot], tb, tb)
    out = x_ref[...] + (gate[:, 0:1] * y0 + gate[:, 1:2] * y1)
    if final:
        out = _rmsnorm(out, gf_ref[...])
    o_ref[...] = out


def _combine(dest_blocked, x1, gates, ybuf, g_final, tb, final):
    S = x1.shape[0]
    grid_spec = pltpu.PrefetchScalarGridSpec(
        num_scalar_prefetch=1,
        grid=(S // tb,),
        in_specs=[pl.BlockSpec((tb, D_MODEL), lambda b, d: (b, 0)),
                  pl.BlockSpec((tb, LANES), lambda b, d: (b, 0)),
                  pl.BlockSpec(memory_space=pl.ANY),
                  pl.BlockSpec((1, D_MODEL), lambda b, d: (0, 0))],
        out_specs=pl.BlockSpec((tb, D_MODEL), lambda b, d: (b, 0)),
        scratch_shapes=[pltpu.VMEM((2, TOP_K * tb * ROW_TILE, LANES), F32), pltpu.SemaphoreType.DMA((2,))],
    )
    return pl.pallas_call(
        functools.partial(_combine_kernel, tb=tb, final=final),
        grid_spec=grid_spec,
        out_shape=jax.ShapeDtypeStruct((S, D_MODEL), F32),
        compiler_params=_params("arbitrary"),
        name="combine",
    )(dest_blocked, x1, gates, ybuf, g_final)


def _pad_cols(w, width):
    return jnp.pad(w, ((0, 0), (0, width - w.shape[1])))


def _swap_cols(w, n_heads):
    w4 = w.reshape(w.shape[0], n_heads, 2, HALF)
    return jnp.stack([-w4[:, :, 1], w4[:, :, 0]], axis=2).reshape(w.shape)


def _proj_weights(w_in):
    w_ret, w_sb, w_fox = N_RET * HEAD_DIM, N_SB * HEAD_DIM, N_FOX * HEAD_DIM
    sizes = [w_ret] * 4 + [w_sb] * 3 + [w_fox] * 3 + [N_FOX]
    offs = np.concatenate([[0], np.cumsum(sizes)])
    q_r, k_r, v_r, gate_r, q_s, k_s, v_s, q_f, k_f, v_f, f_l = [w_in[:, offs[n]:offs[n + 1]] for n in range(11)]
    q_r = q_r * Q_SCALE
    segs = [q_r, _swap_cols(q_r, N_RET), k_r, _swap_cols(k_r, N_RET), v_r, gate_r,
            q_s * (Q_SCALE * LOG2E), k_s, v_s, q_f * (Q_SCALE * LOG2E), k_f, v_f]
    cols = [_pad_cols(s, GROUP_W) for s in segs] + [_pad_cols(f_l, LANES)]
    return jnp.concatenate(cols, axis=1).astype(BF16)


def _rope_tables(S):
    inv = 1.0 / (ROPE_BASE ** (jnp.arange(HALF, dtype=F32) / HALF))
    ang = jnp.arange(S).astype(F32)[:, None] * inv[None, :]
    cos, sin = jnp.cos(ang), jnp.sin(ang)
    return jnp.concatenate([cos] * 4, axis=1), jnp.concatenate([sin] * 4, axis=1)


def _routing_tables(idx, rb):
    T = idx.shape[0]
    A = T * TOP_K
    P = A + N_EXPERTS * rb
    e = idx[:, :TOP_K].reshape(A)
    onehot = (e[:, None] == jnp.arange(N_EXPERTS, dtype=jnp.int32)[None, :]).astype(jnp.int32)
    cs = jnp.cumsum(onehot, axis=0)
    rank = jnp.sum(cs * onehot, axis=1) - 1
    counts = cs[-1]
    padded = ((counts + rb - 1) // rb) * rb
    pend = jnp.cumsum(padded)
    pstart = pend - padded
    dest = (jnp.sum(onehot * pstart[None, :], axis=1) + rank).astype(jnp.int32)
    tok_slot = jnp.zeros((P,), jnp.int32).at[dest].set(jnp.arange(A, dtype=jnp.int32) // TOP_K)
    blk_start = jnp.arange(P // rb, dtype=jnp.int32) * rb
    blk_expert = jnp.minimum(jnp.sum((pend[None, :] <= blk_start[:, None]).astype(jnp.int32), axis=1),
                             N_EXPERTS - 1).astype(jnp.int32)
    nused = (pend[-1] // rb).astype(jnp.int32).reshape(1)
    return dest, tok_slot, blk_expert, nused


def kernel(x, norm_mix, w_in, b_forget, g_ret, g_sb, g_fox, w_out, norm_ffn, w_group, b_group,
           w_router, b_router, w_gate, w_up, w_down, norm_final):
    S = x.shape[1]
    ts = min(512, S)
    tq = min(256, S)
    rb = 256
    tb = min(256, S)
    depth = w_in.shape[0]
    xs = x.reshape(S, D_MODEL)
    cos_t, sin_t = _rope_tables(S)
    row1 = lambda v, width: jnp.pad(v, (0, width - v.shape[0])).reshape(1, width)
    nq = S // tq

    for l in range(depth):
        w_all = _proj_weights(w_in[l])
        qr, kr, vr, gate, qs, ks, vs, qf, kf, vf, fcum = _proj(
            xs, norm_mix[l].reshape(1, D_MODEL), w_all, cos_t, sin_t, row1(b_forget[l], LANES), ts)
        o_ret = _retention(qr, kr, vr, gate, row1(g_ret[l], GROUP_W), tq)
        o_sb = _stick_breaking(qs, ks, vs, row1(g_sb[l], GROUP_W), tq)
        f_pairs = (fcum[:, :2 * PAIRS] * LOG2E).T.reshape(PAIRS, 2, nq, tq).transpose(0, 2, 1, 3)
        o_fx = _fox(qf, kf, vf, f_pairs, row1(g_fox[l], GROUP_W), tq)

        w_o = w_out[l]
        w_ret, w_sb = N_RET * HEAD_DIM, N_SB * HEAD_DIM
        pad_rows = lambda w: jnp.pad(w, ((0, GROUP_W - w.shape[0]), (0, 0)))
        w_out3 = jnp.stack([pad_rows(w_o[:w_ret]), pad_rows(w_o[w_ret:w_ret + w_sb]),
                            pad_rows(w_o[w_ret + w_sb:])]).astype(BF16)
        w_rt = _pad_cols(jnp.concatenate([w_group[l], w_router[l]], axis=1), LANES)
        w_hi = w_rt.astype(BF16)
        w_lo = (w_rt - w_hi.astype(F32)).astype(BF16)
        b_rt = row1(jnp.concatenate([b_group[l], b_router[l]]), LANES)
        g_ffn = norm_ffn[l].reshape(1, D_MODEL)
        x1, x1_tiles, ridx, rgate = _out_router(xs, o_ret, o_sb, o_fx, w_out3, g_ffn, w_hi, w_lo, b_rt, ts)

        dest, tok_slot, blk_expert, nused = _routing_tables(ridx, rb)
        flat = lambda w: w.reshape((depth * N_EXPERTS,) + w.shape[2:])
        ybuf = _experts(blk_expert, tok_slot, nused, x1_tiles, g_ffn, flat(w_gate), flat(w_up), flat(w_down), l, rb)
        dest_blocked = dest.reshape(S // tb, tb, TOP_K).transpose(0, 2, 1).reshape(-1)
        xs = _combine(dest_blocked, x1, rgate, ybuf, norm_final.reshape(1, D_MODEL), tb, l == depth - 1)
    return xs.reshape(x.shape)
```

```python
import functools

import jax
import jax.numpy as jnp
import numpy as np
from jax import lax
from jax.experimental import pallas as pl
from jax.experimental.pallas import tpu as pltpu

F32 = jnp.float32
BF16 = jnp.bfloat16

D_MODEL = 1024
HEAD_DIM = 64
HALF = HEAD_DIM // 2
N_RET, N_SB, N_FOX = 6, 5, 5
PAIRS = 3
GROUP_W = PAIRS * 128
CHUNK = 64
ROPE_BASE = 10000.0
EPS = 1e-6
N_GROUPS, EXPERTS_PER_GROUP = 4, 8
N_EXPERTS = N_GROUPS * EXPERTS_PER_GROUP
TOP_K = 2
D_EXPERT = 512
Q_SCALE = HEAD_DIM ** -0.5

LOG2E = 1.4426950408889634
EXP2_ZERO_BELOW = -159.0
BOUND_SLACK = 1.0 + 2.0 ** -10

LANES = 128
VMEM_LIMIT = 56 * 1024 * 1024

SEG_QR, SEG_KR, SEG_VR, SEG_GATE = 0, 1, 2, 3
SEG_QS, SEG_KS, SEG_VS, SEG_QF, SEG_KF, SEG_VF = 4, 5, 6, 7, 8, 9
N_SEG = 10
PROJ_W = N_SEG * GROUP_W + LANES


def _params(*sem):
    return pltpu.CompilerParams(dimension_semantics=sem, vmem_limit_bytes=VMEM_LIMIT)


def _rmsnorm(x, g):
    return x * lax.rsqrt(jnp.mean(x * x, axis=-1, keepdims=True) + EPS) * g


def _split3(x):
    hi = x.astype(BF16)
    r = x - hi.astype(F32)
    mid = r.astype(BF16)
    lo = (r - mid.astype(F32)).astype(BF16)
    return hi, mid, lo


def _dot(a, b):
    return jnp.dot(a, b, preferred_element_type=F32)


def _dot_nt(a, b):
    return lax.dot_general(a, b, (((1,), (1,)), ((), ())), preferred_element_type=F32)


def _proj_kernel(x_ref, g_ref, w_ref, cos_ref, sin_ref, bf_ref, tri_ref,
                 qr_ref, kr_ref, vr_ref, gate_ref, qs_ref, ks_ref, vs_ref, qf_ref, kf_ref, vf_ref,
                 fcum_ref, carry_ref):
    @pl.when(pl.program_id(0) == 0)
    def _():
        carry_ref[...] = jnp.zeros_like(carry_ref)

    h = _rmsnorm(x_ref[...], g_ref[...]).astype(BF16)

    def seg(k, width=GROUP_W):
        return _dot(h, w_ref[:, k * GROUP_W:k * GROUP_W + width])

    cos, sin = cos_ref[...], sin_ref[...]
    first_half = (lax.broadcasted_iota(jnp.int32, cos.shape, 1) & (HEAD_DIM - 1)) < HALF

    def rotary(x):
        out = []
        for p in range(PAIRS):
            xp = x[:, p * LANES:(p + 1) * LANES]
            swapped = jnp.where(first_half, -pltpu.roll(xp, LANES - HALF, 1), pltpu.roll(xp, HALF, 1))
            out.append(xp * cos + swapped * sin)
        return jnp.concatenate(out, axis=1).astype(BF16)

    qr_ref[...] = rotary(seg(SEG_QR))
    kr_ref[...] = rotary(seg(SEG_KR))
    vr_ref[...] = seg(SEG_VR).astype(BF16)
    gate = seg(SEG_GATE)
    gate_ref[...] = gate * (1.0 / (1.0 + jnp.exp(-gate)))
    qs_ref[...] = seg(SEG_QS).astype(BF16)
    ks_ref[...] = seg(SEG_KS).astype(BF16)
    vs_ref[...] = seg(SEG_VS).astype(BF16)
    qf_ref[...] = seg(SEG_QF).astype(BF16)
    kf_ref[...] = seg(SEG_KF).astype(BF16)
    vf_ref[...] = seg(SEG_VF).astype(BF16)

    fl = seg(N_SEG, LANES) + bf_ref[...]
    logf = jnp.minimum(fl, 0.0) - jnp.log1p(jnp.exp(-jnp.abs(fl)))
    hi, mid, lo = _split3(logf)
    tri = tri_ref[...]
    fc = _dot(tri, hi) + _dot(tri, mid) + _dot(tri, lo) + carry_ref[...]
    fcum_ref[...] = fc
    carry_ref[...] = fc[-1:, :]


def _proj(x, g, w_all, cos_t, sin_t, bf_pad, ts):
    S = x.shape[0]
    tri = jnp.tril(jnp.ones((ts, ts), F32)).astype(BF16)
    row = lambda w: pl.BlockSpec((ts, w), lambda i: (i, 0))
    const = lambda a: pl.BlockSpec(a.shape, lambda i: (0,) * a.ndim)
    bshape = jax.ShapeDtypeStruct((S, GROUP_W), BF16)
    outs = [bshape, bshape, bshape, jax.ShapeDtypeStruct((S, GROUP_W), F32)] + [bshape] * 6 + [
        jax.ShapeDtypeStruct((S, LANES), F32)]
    return pl.pallas_call(
        _proj_kernel,
        grid=(S // ts,),
        in_specs=[row(D_MODEL), const(g), const(w_all), row(LANES), row(LANES), const(bf_pad), const(tri)],
        out_specs=[row(GROUP_W)] * 10 + [row(LANES)],
        out_shape=outs,
        scratch_shapes=[pltpu.VMEM((1, LANES), F32)],
        compiler_params=_params("arbitrary"),
        name="proj",
    )(x, g, w_all, cos_t, sin_t, bf_pad, tri)


def _half_masks(shape):
    lane = lax.broadcasted_iota(jnp.int32, shape, len(shape) - 1)
    return lane < HEAD_DIM, lane >= HEAD_DIM


def _per_head_mean(x, m0):
    s0 = jnp.sum(jnp.where(m0, x, 0.0), axis=1, keepdims=True)
    s1 = jnp.sum(jnp.where(m0, 0.0, x), axis=1, keepdims=True)
    return jnp.where(m0, s0, s1) * (1.0 / HEAD_DIM)


def _ret_kernel(q_ref, k_ref, v_ref, gate_ref, dmask_ref, kdec_ref, qdec_ref, cdec_ref, g_ref,
                o_ref, state_ref):
    @pl.when(pl.program_id(0) == 0)
    def _():
        state_ref[...] = jnp.zeros_like(state_ref)

    m0, m1 = _half_masks((q_ref.shape[0], LANES))
    for p in range(PAIRS):
        lanes = slice(p * LANES, (p + 1) * LANES)
        q, k, v = q_ref[:, lanes], k_ref[:, lanes], v_ref[:, lanes]
        state = state_ref[p]
        qd = (q.astype(F32) * qdec_ref[p]).astype(BF16)
        o = _dot(qd, state.astype(BF16))
        for hh, m in enumerate((m0, m1)):
            s = _dot_nt(jnp.where(m, q, jnp.zeros_like(q)), k)
            oh = _dot((s * dmask_ref[2 * p + hh]).astype(BF16), v)
            o = o + jnp.where(m, oh, 0.0)
        kdt = (k.astype(F32) * kdec_ref[p]).T.astype(BF16)
        u = _dot(kdt, v)
        r = lax.broadcasted_iota(jnp.int32, u.shape, 0) < HEAD_DIM
        c = lax.broadcasted_iota(jnp.int32, u.shape, 1) < HEAD_DIM
        state_ref[p] = state * cdec_ref[p] + jnp.where(r == c, u, 0.0)
        cen = o - _per_head_mean(o, m0)
        var = _per_head_mean(cen * cen, m0)
        o_ref[:, lanes] = (cen * lax.rsqrt(var + EPS) * g_ref[:, lanes] * gate_ref[:, lanes]).astype(BF16)


def _ret_tables(B):
    hs = np.arange(N_RET, dtype=np.float64)
    lg = np.log(1.0 - 2.0 ** (-5.0 - hs))
    idx = np.arange(B, dtype=np.float64)
    diff = idx[:, None] - idx[None, :]
    ch = np.arange(B) // CHUNK
    same = ch[:, None] == ch[None, :]
    earlier = ch[None, :] < ch[:, None]
    expo = np.where(same, np.abs(diff), diff)
    dmask = np.where(same | earlier, np.exp(lg[:, None, None] * expo[None]), 0.0)
    kdec = np.exp(lg[:, None] * (B - 1.0 - idx)[None])
    qdec = np.exp(lg[:, None] * (idx + 1.0)[None])
    cdec = np.exp(lg * B)
    lane_head = np.arange(LANES) // HEAD_DIM
    kdec_p = np.stack([kdec[2 * p + lane_head].T for p in range(PAIRS)])
    qdec_p = np.stack([qdec[2 * p + lane_head].T for p in range(PAIRS)])
    cdec_p = np.stack([np.broadcast_to(cdec[2 * p + lane_head][None, :], (LANES, LANES)) for p in range(PAIRS)])
    f = lambda a: jnp.asarray(a, F32)
    return f(dmask), f(kdec_p), f(qdec_p), f(cdec_p)


def _retention(qr, kr, vr, gate, g_pad, B):
    S = qr.shape[0]
    dmask, kdec, qdec, cdec = _ret_tables(B)
    blk = pl.BlockSpec((B, GROUP_W), lambda b: (b, 0))
    const = lambda a: pl.BlockSpec(a.shape, lambda b: (0,) * a.ndim)
    return pl.pallas_call(
        _ret_kernel,
        grid=(S // B,),
        in_specs=[blk, blk, blk, blk, const(dmask), const(kdec), const(qdec), const(cdec), const(g_pad)],
        out_specs=blk,
        out_shape=jax.ShapeDtypeStruct((S, GROUP_W), BF16),
        scratch_shapes=[pltpu.VMEM((PAIRS, LANES, LANES), F32)],
        compiler_params=_params("arbitrary"),
        name="retention",
    )(qr, kr, vr, gate, dmask, kdec, qdec, cdec, g_pad)


def _head_rms_out(acc0, acc1, g, m0):
    o = jnp.where(m0, acc0, acc1)
    ms = _per_head_mean(o * o, m0)
    return (o * lax.rsqrt(ms + EPS) * g).astype(BF16)


def _pair_rows(q, m0):
    zero = jnp.zeros_like(q)
    return jnp.concatenate([jnp.where(m0, q, zero), jnp.where(m0, zero, q)], axis=0)


def _rows2(a0, a1, tq):
    w = a0.shape[1]
    return jnp.concatenate([jnp.broadcast_to(a0, (tq, w)), jnp.broadcast_to(a1, (tq, w))], axis=0)


def _max_key_norm(k_ref, kmax_ref, tq, m0):
    def blk(j, mx):
        kk = k_ref[pl.ds(pl.multiple_of(j * tq, tq), tq), :].astype(F32)
        n2 = _per_head_mean(kk * kk, m0) * float(HEAD_DIM)
        return jnp.maximum(mx, jnp.max(n2, axis=0, keepdims=True))

    n2max = lax.fori_loop(0, k_ref.shape[0] // tq, blk, jnp.zeros((1, LANES), F32))
    kmax_ref[...] = jnp.sqrt(n2max)


def _score_bound(qm, kmax_ref, tq):
    qf = qm.astype(F32)
    qn = jnp.sqrt(jnp.sum(qf * qf, axis=1, keepdims=True))
    kn = _rows2(kmax_ref[:, 0:1], kmax_ref[:, HEAD_DIM:HEAD_DIM + 1], tq)
    return qn * kn * BOUND_SLACK


def _sb_kernel(q_ref, k_ref, v_ref, tri_ref, g_ref, o_ref, acc_ref, kmax_ref, *, tq):
    i = pl.program_id(1)
    m0, _ = _half_masks((tq, LANES))

    @pl.when(i == 0)
    def _():
        _max_key_norm(k_ref, kmax_ref, tq, m0)

    qm = _pair_rows(q_ref[...], m0)
    zbound = _score_bound(qm, kmax_ref, tq)
    row = lax.broadcasted_iota(jnp.int32, (2 * tq, tq), 0) & (tq - 1)
    col = lax.broadcasted_iota(jnp.int32, (2 * tq, tq), 1)
    acc_ref[...] = jnp.zeros_like(acc_ref)

    def step(j, carry, diag):
        ks = k_ref[pl.ds(pl.multiple_of(j * tq, tq), tq), :]
        vs = v_ref[pl.ds(pl.multiple_of(j * tq, tq), tq), :]
        tri = tri_ref[...]
        z = _dot_nt(qm, ks)
        l1m = -(jnp.maximum(z, 0.0) + jnp.log2(1.0 + jnp.exp2(-jnp.abs(z))))
        if diag:
            l1m = jnp.where(col < row, l1m, 0.0)
        hi = l1m.astype(BF16)
        lo = (l1m - hi.astype(F32)).astype(BF16)
        csum = _dot(hi, tri) + _dot(lo, tri) + carry
        a = jnp.exp2(z + csum)
        if diag:
            a = jnp.where(col < row, a, 0.0)
        acc_ref[...] += _dot(a.astype(BF16), vs)
        return csum[:, 0:1]

    def alive(carry):
        return jnp.max(carry + zbound) >= EXP2_ZERO_BELOW

    def body(st):
        j, carry, _ = st
        carry = step(j, carry, False)
        return j - 1, carry, alive(carry)

    carry = step(i, jnp.zeros((2 * tq, 1), F32), True)
    lax.while_loop(lambda st: jnp.logical_and(st[0] >= 0, st[2]), body, (i - 1, carry, alive(carry)))
    o_ref[...] = _head_rms_out(acc_ref[:tq], acc_ref[tq:], g_ref[...], m0)


def _stick_breaking(q, k, v, g_pad, tq):
    S = q.shape[0]
    tri = jnp.tril(jnp.ones((tq, tq), F32)).astype(BF16)
    return pl.pallas_call(
        functools.partial(_sb_kernel, tq=tq),
        grid=(PAIRS, S // tq),
        in_specs=[pl.BlockSpec((tq, LANES), lambda p, i: (i, p)),
                  pl.BlockSpec((S, LANES), lambda p, i: (0, p)),
                  pl.BlockSpec((S, LANES), lambda p, i: (0, p)),
                  pl.BlockSpec((tq, tq), lambda p, i: (0, 0)),
                  pl.BlockSpec((1, LANES), lambda p, i: (0, p))],
        out_specs=pl.BlockSpec((tq, LANES), lambda p, i: (i, p)),
        out_shape=jax.ShapeDtypeStruct((S, GROUP_W), BF16),
        scratch_shapes=[pltpu.VMEM((2 * tq, LANES), F32), pltpu.VMEM((1, LANES), F32)],
        compiler_params=_params("arbitrary", "arbitrary"),
        name="stick_breaking",
    )(q, k, v, tri, g_pad)


def _fox_kernel(q_ref, k_ref, v_ref, f_ref, g_ref, o_ref, acc_ref, out_ref, kmax_ref, s_ref, *, n_heads, tk):
    p, i = pl.program_id(0), pl.program_id(1)
    tq = 2 * tk
    m0, m1 = _half_masks((tq, LANES))

    @pl.when(i == 0)
    def _():
        _max_key_norm(k_ref, kmax_ref, tk, _half_masks((tk, LANES))[0])

    row = lax.broadcasted_iota(jnp.int32, (tq, tk), 0)
    col = lax.broadcasted_iota(jnp.int32, (tq, tk), 1)
    mask_lo = (col <= row) | (row >= tk)
    mask_hi = col <= row - tk
    out_ref[...] = jnp.zeros_like(out_ref)

    def kblock(ref, j):
        return ref[pl.ds(pl.multiple_of(jnp.maximum(j, 0) * tk, tk), tk), :]

    def head(hh, m):
        q = q_ref[...]
        qm = jnp.where(m, q, jnp.zeros_like(q))
        qf = qm.astype(F32)
        kmax = kmax_ref[:, hh * HEAD_DIM:hh * HEAD_DIM + 1]
        sbound = jnp.sqrt(jnp.sum(qf * qf, axis=1, keepdims=True)) * kmax * BOUND_SLACK
        fref = f_ref[0, 2 * i, hh:hh + 1, 0:1]
        acc_ref[...] = jnp.zeros_like(acc_ref)

        def scores(j, slot):
            s_ref[slot] = _dot_nt(qm, kblock(k_ref, j))

        def step(j, slot, nxt, m_run, mask=None):
            scores(nxt, 1 - slot)
            vs = kblock(v_ref, j)
            vaug = jnp.where(_half_masks((tk, LANES))[hh], vs, jnp.ones_like(vs))
            bias = jnp.where(j >= 0, fref - f_ref[0, jnp.maximum(j, 0), hh:hh + 1, :], -jnp.inf)
            t = s_ref[slot] + bias
            if mask is not None:
                t = jnp.where(mask, t, -jnp.inf)
            m_new = jnp.maximum(m_run, jnp.max(t, axis=1, keepdims=True))
            pr = jnp.exp2(t - m_new).astype(BF16)
            acc_ref[...] = jnp.exp2(m_run - m_new) * acc_ref[...] + _dot(pr, vaug)
            return m_new

        def alive(j, m_run):
            fend = f_ref[0, jnp.maximum(j, 0), hh:hh + 1, tk - 1:tk]
            return jnp.max(sbound - m_run + (fref - fend)) >= EXP2_ZERO_BELOW

        def body(st):
            j, m_run, _ = st
            m_run = step(j, 1, j - 1, m_run)
            m_run = step(j - 1, 0, j - 2, m_run)
            return j - 2, m_run, alive(j - 2, m_run)

        scores(2 * i, 1)
        m_run = step(2 * i, 1, 2 * i + 1, jnp.full((tq, 1), -jnp.inf, F32), mask_lo)
        m_run = step(2 * i + 1, 0, 2 * i - 1, m_run, mask_hi)
        j0 = 2 * i - 1
        lax.while_loop(lambda st: jnp.logical_and(st[0] >= 0, st[2]), body, (j0, m_run, alive(j0, m_run)))
        acc = acc_ref[...]
        out_ref[hh] = acc / jnp.max(jnp.where(m, -jnp.inf, acc), axis=1, keepdims=True)

    head(0, m0)

    @pl.when(2 * p + 1 < n_heads)
    def _():
        head(1, m1)

    o_ref[...] = _head_rms_out(out_ref[0], out_ref[1], g_ref[...], m0)


def _fox(q, k, v, f_pairs, g_pad, tk):
    S = q.shape[0]
    tq = 2 * tk
    return pl.pallas_call(
        functools.partial(_fox_kernel, n_heads=N_FOX, tk=tk),
        grid=(PAIRS, S // tq),
        in_specs=[pl.BlockSpec((tq, LANES), lambda p, i: (i, p)),
                  pl.BlockSpec((S, LANES), lambda p, i: (0, p)),
                  pl.BlockSpec((S, LANES), lambda p, i: (0, p)),
                  pl.BlockSpec((1, S // tk, 2, tk), lambda p, i: (p, 0, 0, 0)),
                  pl.BlockSpec((1, LANES), lambda p, i: (0, p))],
        out_specs=pl.BlockSpec((tq, LANES), lambda p, i: (i, p)),
        out_shape=jax.ShapeDtypeStruct((S, GROUP_W), BF16),
        scratch_shapes=[pltpu.VMEM((tq, LANES), F32), pltpu.VMEM((2, tq, LANES), F32),
                        pltpu.VMEM((1, LANES), F32), pltpu.VMEM((2, tq, tk), F32)],
        compiler_params=_params("arbitrary", "arbitrary"),
        name="fox",
    )(q, k, v, f_pairs, g_pad)


def _out_router_kernel(x_ref, ret_ref, sb_ref, fx_ref, w_ref, g_ref, whi_ref, wlo_ref, b_ref,
                       x1_ref, x1t_ref, idx_ref, gate_ref):
    mix = _dot(ret_ref[...], w_ref[0]) + _dot(sb_ref[...], w_ref[1]) + _dot(fx_ref[...], w_ref[2])
    x1 = x_ref[...] + mix
    x1_ref[...] = x1
    _store_row_tiles(x1t_ref, x1, x1.shape[0])
    h = _rmsnorm(x1, g_ref[...])
    hi = h.astype(BF16)
    lo = (h - hi.astype(F32)).astype(BF16)
    whi = whi_ref[...]
    logits = _dot(hi, whi) + _dot(lo, whi) + _dot(hi, wlo_ref[...]) + b_ref[...]
    lane = lax.broadcasted_iota(jnp.int32, logits.shape, 1).astype(F32)
    neg = -jnp.inf
    big = float(LANES)

    def top(vals):
        v = jnp.max(vals, axis=1, keepdims=True)
        at = jnp.min(jnp.where(vals == v, lane, big), axis=1, keepdims=True)
        return v, at

    lg = jnp.where(lane < N_GROUPS, logits, neg)
    gmax, grp = top(lg)
    p_grp = 1.0 / jnp.sum(jnp.exp(lg - gmax), axis=1, keepdims=True)
    lo_lane = N_GROUPS + EXPERTS_PER_GROUP * grp
    le = jnp.where((lane >= lo_lane) & (lane < lo_lane + EXPERTS_PER_GROUP), logits, neg)
    v1, i1 = top(le)
    v2, i2 = top(jnp.where(lane == i1, neg, le))
    e21 = jnp.exp(v2 - v1)
    g1 = p_grp / (1.0 + e21)
    g2 = p_grp * e21 / (1.0 + e21)
    idx = jnp.where(lane == 0.0, i1, i2) - float(N_GROUPS)
    idx_ref[...] = jnp.where(lane < TOP_K, idx, 0.0).astype(jnp.int32)
    gate_ref[...] = jnp.where(lane == 0.0, g1, jnp.where(lane == 1.0, g2, 0.0))


def _out_router(x, o_ret, o_sb, o_fx, w_out3, g, w_hi, w_lo, b_rt, ts):
    S = x.shape[0]
    row = lambda w: pl.BlockSpec((ts, w), lambda i: (i, 0))
    const = lambda a: pl.BlockSpec(a.shape, lambda i: (0,) * a.ndim)
    return pl.pallas_call(
        _out_router_kernel,
        grid=(S // ts,),
        in_specs=[row(D_MODEL), row(GROUP_W), row(GROUP_W), row(GROUP_W), const(w_out3), const(g),
                  const(w_hi), const(w_lo), const(b_rt)],
        out_specs=[row(D_MODEL), pl.BlockSpec((ts * ROW_TILE, LANES), lambda i: (i, 0)), row(LANES), row(LANES)],
        out_shape=[jax.ShapeDtypeStruct((S, D_MODEL), F32), jax.ShapeDtypeStruct((S * ROW_TILE, LANES), F32),
                   jax.ShapeDtypeStruct((S, LANES), jnp.int32), jax.ShapeDtypeStruct((S, LANES), F32)],
        compiler_params=_params("arbitrary"),
        name="out_router",
    )(x, o_ret, o_sb, o_fx, w_out3, g, w_hi, w_lo, b_rt)


ROW_TILE = D_MODEL // LANES
GATHER_BUFS = 3


def _store_row_tiles(ref, val, n):
    for c in range(ROW_TILE):
        ref[pl.ds(c, n, stride=ROW_TILE), :] = val[:, c * LANES:(c + 1) * LANES]


def _load_row_tiles(ref, first, n):
    return jnp.concatenate([ref[pl.ds(first * ROW_TILE + c, n, stride=ROW_TILE), :] for c in range(ROW_TILE)], axis=1)


def _tile_gather(src_hbm, dst, sem, idx_ref, base, n):
    def one(r):
        src_row = pl.multiple_of(idx_ref[base + r] * ROW_TILE, ROW_TILE)
        pltpu.make_async_copy(src_hbm.at[pl.ds(src_row, ROW_TILE), :], dst.at[pl.ds(r * ROW_TILE, ROW_TILE), :],
                              sem).start()

    def start_loop():
        lax.fori_loop(0, n, lambda r, c: (one(r), c)[1], 0, unroll=8)

    def start_unrolled():
        for r in range(n):
            one(r)

    def wait():
        pltpu.make_async_copy(src_hbm.at[pl.ds(0, n * ROW_TILE), :], dst, sem).wait()

    return start_loop, start_unrolled, wait


def _expert_kernel(be_ref, tok_ref, nused_ref, x_hbm, g_ref, wg_ref, wu_ref, wd_ref, y_ref,
                   xbuf, wg_bf, wu_bf, wd_bf, sem, *, rb):
    b = pl.program_id(0)
    nused = nused_ref[0]
    slot = b % GATHER_BUFS

    def gather(blk, s):
        return _tile_gather(x_hbm, xbuf.at[s], sem.at[s], tok_ref, jnp.minimum(blk, nused - 1) * rb, rb)

    @pl.when(b == 0)
    def _():
        for s in range(GATHER_BUFS - 1):
            gather(s, s)[0]()

    @pl.when(b < nused)
    def _():
        @pl.when(jnp.logical_or(b == 0, be_ref[b] != be_ref[jnp.maximum(b - 1, 0)]))
        def _():
            wg_bf[...] = wg_ref[0].astype(BF16)
            wu_bf[...] = wu_ref[0].astype(BF16)
            wd_bf[...] = wd_ref[0].astype(BF16)

        gather(b, slot)[2]()
        gather(b + GATHER_BUFS - 1, (b + GATHER_BUFS - 1) % GATHER_BUFS)[1]()
        h = _rmsnorm(_load_row_tiles(xbuf.at[slot], 0, rb), g_ref[...]).astype(BF16)
        gt = _dot(h, wg_bf[...])
        up = _dot(h, wu_bf[...])
        hid = gt * (1.0 / (1.0 + jnp.exp(-gt))) * up
        _store_row_tiles(y_ref, _dot(hid.astype(BF16), wd_bf[...]), rb)

    @pl.when(b == nused)
    def _():
        for s in range(GATHER_BUFS - 1):
            gather(b, (b + s) % GATHER_BUFS)[2]()

    @pl.when(b >= nused)
    def _():
        y_ref[...] = jnp.zeros_like(y_ref)


def _experts(blk_expert, tok_slot, nused, x1, g, w_gate, w_up, w_down, layer, rb):
    P = tok_slot.shape[0]
    nblk = P // rb
    live = lambda b, nu: jnp.minimum(b, nu[0] - 1)
    wsel = lambda b, be, tk, nu: (layer * N_EXPERTS + be[live(b, nu)], 0, 0)
    grid_spec = pltpu.PrefetchScalarGridSpec(
        num_scalar_prefetch=3,
        grid=(nblk,),
        in_specs=[pl.BlockSpec(memory_space=pl.ANY),
                  pl.BlockSpec((1, D_MODEL), lambda b, be, tk, nu: (0, 0)),
                  pl.BlockSpec((1, D_MODEL, D_EXPERT), wsel),
                  pl.BlockSpec((1, D_MODEL, D_EXPERT), wsel),
                  pl.BlockSpec((1, D_EXPERT, D_MODEL), wsel)],
        out_specs=pl.BlockSpec((rb * ROW_TILE, LANES), lambda b, be, tk, nu: (b, 0)),
        scratch_shapes=[pltpu.VMEM((GATHER_BUFS, rb * ROW_TILE, LANES), F32),
                        pltpu.VMEM((D_MODEL, D_EXPERT), BF16), pltpu.VMEM((D_MODEL, D_EXPERT), BF16),
                        pltpu.VMEM((D_EXPERT, D_MODEL), BF16), pltpu.SemaphoreType.DMA((GATHER_BUFS,))],
    )
    return pl.pallas_call(
        functools.partial(_expert_kernel, rb=rb),
        grid_spec=grid_spec,
        out_shape=jax.ShapeDtypeStruct((P * ROW_TILE, LANES), F32),
        compiler_params=_params("arbitrary"),
        name="experts",
    )(blk_expert, tok_slot, nused, x1, g, w_gate, w_up, w_down)


def _combine_kernel(dest_ref, x_ref, gate_ref, y_hbm, gf_ref, o_ref, ybuf, sem, *, tb, final):
    b = pl.program_id(0)
    nb = pl.num_programs(0)
    slot = b % 2

    def gather(blk, s):
        return _tile_gather(y_hbm, ybuf.at[s], sem.at[s], dest_ref, blk * (TOP_K * tb), TOP_K * tb)

    @pl.when(b == 0)
    def _():
        gather(0, 0)[0]()

    @pl.when(b + 1 < nb)
    def _():
        gather(b + 1, 1 - slot)[0]()

    gather(b, slot)[2]()
    gate = gate_ref[...]
    y0 = _load_row_tiles(ybuf.at[slot], 0, tb)
    y1 = _load_row_tiles(ybuf.at[slot], tb, tb)
    out = x_ref[...] + (gate[:, 0:1] * y0 + gate[:, 1:2] * y1)
    if final:
        out = _rmsnorm(out, gf_ref[...])
    o_ref[...] = out


def _combine(dest_blocked, x1, gates, ybuf, g_final, tb, final):
    S = x1.shape[0]
    grid_spec = pltpu.PrefetchScalarGridSpec(
        num_scalar_prefetch=1,
        grid=(S // tb,),
        in_specs=[pl.BlockSpec((tb, D_MODEL), lambda b, d: (b, 0)),
                  pl.BlockSpec((tb, LANES), lambda b, d: (b, 0)),
                  pl.BlockSpec(memory_space=pl.ANY),
                  pl.BlockSpec((1, D_MODEL), lambda b, d: (0, 0))],
        out_specs=pl.BlockSpec((tb, D_MODEL), lambda b, d: (b, 0)),
        scratch_shapes=[pltpu.VMEM((2, TOP_K * tb * ROW_TILE, LANES), F32), pltpu.SemaphoreType.DMA((2,))],
    )
    return pl.pallas_call(
        functools.partial(_combine_kernel, tb=tb, final=final),
        grid_spec=grid_spec,
        out_shape=jax.ShapeDtypeStruct((S, D_MODEL), F32),
        compiler_params=_params("arbitrary"),
        name="combine",
    )(dest_blocked, x1, gates, ybuf, g_final)


def _pad_cols(w, width):
    return jnp.pad(w, ((0, 0), (0, width - w.shape[1])))


def _proj_weights(w_in):
    w_ret, w_sb, w_fox = N_RET * HEAD_DIM, N_SB * HEAD_DIM, N_FOX * HEAD_DIM
    sizes = [w_ret] * 4 + [w_sb] * 3 + [w_fox] * 3 + [N_FOX]
    offs = np.concatenate([[0], np.cumsum(sizes)])
    q_r, k_r, v_r, gate_r, q_s, k_s, v_s, q_f, k_f, v_f, f_l = [w_in[:, offs[n]:offs[n + 1]] for n in range(11)]
    q_r = q_r * Q_SCALE
    segs = [q_r, k_r, v_r, gate_r,
            q_s * (Q_SCALE * LOG2E), k_s, v_s, q_f * (Q_SCALE * LOG2E), k_f, v_f]
    cols = [_pad_cols(s, GROUP_W) for s in segs] + [_pad_cols(f_l, LANES)]
    return jnp.concatenate(cols, axis=1).astype(BF16)


def _rope_tables(S):
    inv = 1.0 / (ROPE_BASE ** (jnp.arange(HALF, dtype=F32) / HALF))
    ang = jnp.arange(S).astype(F32)[:, None] * inv[None, :]
    cos, sin = jnp.cos(ang), jnp.sin(ang)
    return jnp.concatenate([cos] * 4, axis=1), jnp.concatenate([sin] * 4, axis=1)


def _routing_tables(idx, rb):
    T = idx.shape[0]
    A = T * TOP_K
    P = A + N_EXPERTS * rb
    e = idx[:, :TOP_K].reshape(A)
    onehot = (e[:, None] == jnp.arange(N_EXPERTS, dtype=jnp.int32)[None, :]).astype(jnp.int32)
    cs = jnp.cumsum(onehot, axis=0)
    rank = jnp.sum(cs * onehot, axis=1) - 1
    counts = cs[-1]
    padded = ((counts + rb - 1) // rb) * rb
    pend = jnp.cumsum(padded)
    pstart = pend - padded
    dest = (jnp.sum(onehot * pstart[None, :], axis=1) + rank).astype(jnp.int32)
    tok_slot = jnp.zeros((P,), jnp.int32).at[dest].set(jnp.arange(A, dtype=jnp.int32) // TOP_K)
    blk_start = jnp.arange(P // rb, dtype=jnp.int32) * rb
    blk_expert = jnp.minimum(jnp.sum((pend[None, :] <= blk_start[:, None]).astype(jnp.int32), axis=1),
                             N_EXPERTS - 1).astype(jnp.int32)
    nused = (pend[-1] // rb).astype(jnp.int32).reshape(1)
    return dest, tok_slot, blk_expert, nused


def kernel(x, norm_mix, w_in, b_forget, g_ret, g_sb, g_fox, w_out, norm_ffn, w_group, b_group,
           w_router, b_router, w_gate, w_up, w_down, norm_final):
    S = x.shape[1]
    ts = min(512, S)
    tq = min(256, S)
    rb = 256
    tb = min(256, S)
    depth = w_in.shape[0]
    xs = x.reshape(S, D_MODEL)
    cos_t, sin_t = _rope_tables(S)
    row1 = lambda v, width: jnp.pad(v, (0, width - v.shape[0])).reshape(1, width)
    nq = S // tq

    for l in range(depth):
        w_all = _proj_weights(w_in[l])
        qr, kr, vr, gate, qs, ks, vs, qf, kf, vf, fcum = _proj(
            xs, norm_mix[l].reshape(1, D_MODEL), w_all, cos_t, sin_t, row1(b_forget[l], LANES), ts)
        o_ret = _retention(qr, kr, vr, gate, row1(g_ret[l], GROUP_W), tq)
        o_sb = _stick_breaking(qs, ks, vs, row1(g_sb[l], GROUP_W), tq)
        f_pairs = (fcum[:, :2 * PAIRS] * LOG2E).T.reshape(PAIRS, 2, nq, tq).transpose(0, 2, 1, 3)
        o_fx = _fox(qf, kf, vf, f_pairs, row1(g_fox[l], GROUP_W), tq)

        w_o = w_out[l]
        w_ret, w_sb = N_RET * HEAD_DIM, N_SB * HEAD_DIM
        pad_rows = lambda w: jnp.pad(w, ((0, GROUP_W - w.shape[0]), (0, 0)))
        w_out3 = jnp.stack([pad_rows(w_o[:w_ret]), pad_rows(w_o[w_ret:w_ret + w_sb]),
                            pad_rows(w_o[w_ret + w_sb:])]).astype(BF16)
        w_rt = _pad_cols(jnp.concatenate([w_group[l], w_router[l]], axis=1), LANES)
        w_hi = w_rt.astype(BF16)
        w_lo = (w_rt - w_hi.astype(F32)).astype(BF16)
        b_rt = row1(jnp.concatenate([b_group[l], b_router[l]]), LANES)
        g_ffn = norm_ffn[l].reshape(1, D_MODEL)
        x1, x1_tiles, ridx, rgate = _out_router(xs, o_ret, o_sb, o_fx, w_out3, g_ffn, w_hi, w_lo, b_rt, ts)

        dest, tok_slot, blk_expert, nused = _routing_tables(ridx, rb)
        flat = lambda w: w.reshape((depth * N_EXPERTS,) + w.shape[2:])
        ybuf = _experts(blk_expert, tok_slot, nused, x1_tiles, g_ffn, flat(w_gate), flat(w_up), flat(w_down), l, rb)
        dest_blocked = dest.reshape(S // tb, tb, TOP_K).transpose(0, 2, 1).reshape(-1)
        xs = _combine(dest_blocked, x1, rgate, ybuf, norm_final.reshape(1, D_MODEL), tb, l == depth - 1)
    return xs.reshape(x.shape)
```

```python
import functools

import jax
import jax.numpy as jnp
import numpy as np
from jax import lax
from jax.experimental import pallas as pl
from jax.experimental.pallas import tpu as pltpu

F32 = jnp.float32
BF16 = jnp.bfloat16

D_MODEL = 1024
HEAD_DIM = 64
HALF = HEAD_DIM // 2
N_RET, N_SB, N_FOX = 6, 5, 5
PAIRS = 3
GROUP_W = PAIRS * 128
CHUNK = 64
ROPE_BASE = 10000.0
EPS = 1e-6
N_GROUPS, EXPERTS_PER_GROUP = 4, 8
N_EXPERTS = N_GROUPS * EXPERTS_PER_GROUP
TOP_K = 2
D_EXPERT = 512
Q_SCALE = HEAD_DIM ** -0.5

LOG2E = 1.4426950408889634
EXP2_ZERO_BELOW = -159.0
BOUND_SLACK = 1.0 + 2.0 ** -10

LANES = 128
VMEM_LIMIT = 56 * 1024 * 1024

SEG_QR, SEG_KR, SEG_VR, SEG_GATE = 0, 1, 2, 3
SEG_QS, SEG_KS, SEG_VS, SEG_QF, SEG_KF, SEG_VF = 4, 5, 6, 7, 8, 9
N_SEG = 10
PROJ_W = N_SEG * GROUP_W + LANES


def _params(*sem):
    return pltpu.CompilerParams(dimension_semantics=sem, vmem_limit_bytes=VMEM_LIMIT)


def _rmsnorm(x, g):
    return x * lax.rsqrt(jnp.mean(x * x, axis=-1, keepdims=True) + EPS) * g


def _split3(x):
    hi = x.astype(BF16)
    r = x - hi.astype(F32)
    mid = r.astype(BF16)
    lo = (r - mid.astype(F32)).astype(BF16)
    return hi, mid, lo


def _dot(a, b):
    return jnp.dot(a, b, preferred_element_type=F32)


def _dot_nt(a, b):
    return lax.dot_general(a, b, (((1,), (1,)), ((), ())), preferred_element_type=F32)


def _proj_kernel(x_ref, g_ref, w_ref, cos_ref, sin_ref, bf_ref, tri_ref,
                 qr_ref, kr_ref, vr_ref, gate_ref, qs_ref, ks_ref, vs_ref, qf_ref, kf_ref, vf_ref,
                 fcum_ref, carry_ref):
    @pl.when(pl.program_id(0) == 0)
    def _():
        carry_ref[...] = jnp.zeros_like(carry_ref)

    h = _rmsnorm(x_ref[...], g_ref[...]).astype(BF16)

    def seg(k, width=GROUP_W):
        return _dot(h, w_ref[:, k * GROUP_W:k * GROUP_W + width])

    cos, sin = cos_ref[...], sin_ref[...]
    first_half = (lax.broadcasted_iota(jnp.int32, cos.shape, 1) & (HEAD_DIM - 1)) < HALF

    def rotary(x):
        out = []
        for p in range(PAIRS):
            xp = x[:, p * LANES:(p + 1) * LANES]
            swapped = jnp.where(first_half, -pltpu.roll(xp, LANES - HALF, 1), pltpu.roll(xp, HALF, 1))
            out.append(xp * cos + swapped * sin)
        return jnp.concatenate(out, axis=1).astype(BF16)

    qr_ref[...] = rotary(seg(SEG_QR))
    kr_ref[...] = rotary(seg(SEG_KR))
    vr_ref[...] = seg(SEG_VR).astype(BF16)
    gate = seg(SEG_GATE)
    gate_ref[...] = gate * (1.0 / (1.0 + jnp.exp(-gate)))
    qs_ref[...] = seg(SEG_QS).astype(BF16)
    ks_ref[...] = seg(SEG_KS).astype(BF16)
    vs_ref[...] = seg(SEG_VS).astype(BF16)
    qf_ref[...] = seg(SEG_QF).astype(BF16)
    kf_ref[...] = seg(SEG_KF).astype(BF16)
    vf_ref[...] = seg(SEG_VF).astype(BF16)

    fl = seg(N_SEG, LANES) + bf_ref[...]
    logf = jnp.minimum(fl, 0.0) - jnp.log1p(jnp.exp(-jnp.abs(fl)))
    hi, mid, lo = _split3(logf)
    tri = tri_ref[...]
    fc = _dot(tri, hi) + _dot(tri, mid) + _dot(tri, lo) + carry_ref[...]
    fcum_ref[...] = fc
    carry_ref[...] = fc[-1:, :]


def _proj(x, g, w_all, cos_t, sin_t, bf_pad, ts):
    S = x.shape[0]
    tri = jnp.tril(jnp.ones((ts, ts), F32)).astype(BF16)
    row = lambda w: pl.BlockSpec((ts, w), lambda i: (i, 0))
    const = lambda a: pl.BlockSpec(a.shape, lambda i: (0,) * a.ndim)
    bshape = jax.ShapeDtypeStruct((S, GROUP_W), BF16)
    outs = [bshape, bshape, bshape, jax.ShapeDtypeStruct((S, GROUP_W), F32)] + [bshape] * 6 + [
        jax.ShapeDtypeStruct((S, LANES), F32)]
    return pl.pallas_call(
        _proj_kernel,
        grid=(S // ts,),
        in_specs=[row(D_MODEL), const(g), const(w_all), row(LANES), row(LANES), const(bf_pad), const(tri)],
        out_specs=[row(GROUP_W)] * 10 + [row(LANES)],
        out_shape=outs,
        scratch_shapes=[pltpu.VMEM((1, LANES), F32)],
        compiler_params=_params("arbitrary"),
        name="proj",
    )(x, g, w_all, cos_t, sin_t, bf_pad, tri)


def _half_masks(shape):
    lane = lax.broadcasted_iota(jnp.int32, shape, len(shape) - 1)
    return lane < HEAD_DIM, lane >= HEAD_DIM


def _per_head_mean(x, m0):
    s0 = jnp.sum(jnp.where(m0, x, 0.0), axis=1, keepdims=True)
    s1 = jnp.sum(jnp.where(m0, 0.0, x), axis=1, keepdims=True)
    return jnp.where(m0, s0, s1) * (1.0 / HEAD_DIM)


def _ret_kernel(q_ref, k_ref, v_ref, gate_ref, dmask_ref, kdec_ref, qdec_ref, cdec_ref, g_ref,
                o_ref, state_ref):
    @pl.when(pl.program_id(0) == 0)
    def _():
        state_ref[...] = jnp.zeros_like(state_ref)

    m0, m1 = _half_masks((q_ref.shape[0], LANES))
    for p in range(PAIRS):
        lanes = slice(p * LANES, (p + 1) * LANES)
        q, k, v = q_ref[:, lanes], k_ref[:, lanes], v_ref[:, lanes]
        state = state_ref[p]
        qd = (q.astype(F32) * qdec_ref[p]).astype(BF16)
        o = _dot(qd, state.astype(BF16))
        for hh, m in enumerate((m0, m1)):
            s = _dot_nt(jnp.where(m, q, jnp.zeros_like(q)), k)
            oh = _dot((s * dmask_ref[2 * p + hh]).astype(BF16), v)
            o = o + jnp.where(m, oh, 0.0)
        kdt = (k.astype(F32) * kdec_ref[p]).T.astype(BF16)
        u = _dot(kdt, v)
        r = lax.broadcasted_iota(jnp.int32, u.shape, 0) < HEAD_DIM
        c = lax.broadcasted_iota(jnp.int32, u.shape, 1) < HEAD_DIM
        state_ref[p] = state * cdec_ref[p] + jnp.where(r == c, u, 0.0)
        cen = o - _per_head_mean(o, m0)
        var = _per_head_mean(cen * cen, m0)
        o_ref[:, lanes] = (cen * lax.rsqrt(var + EPS) * g_ref[:, lanes] * gate_ref[:, lanes]).astype(BF16)


def _ret_tables(B):
    hs = np.arange(N_RET, dtype=np.float64)
    lg = np.log(1.0 - 2.0 ** (-5.0 - hs))
    idx = np.arange(B, dtype=np.float64)
    diff = idx[:, None] - idx[None, :]
    ch = np.arange(B) // CHUNK
    same = ch[:, None] == ch[None, :]
    earlier = ch[None, :] < ch[:, None]
    expo = np.where(same, np.abs(diff), diff)
    dmask = np.where(same | earlier, np.exp(lg[:, None, None] * expo[None]), 0.0)
    kdec = np.exp(lg[:, None] * (B - 1.0 - idx)[None])
    qdec = np.exp(lg[:, None] * (idx + 1.0)[None])
    cdec = np.exp(lg * B)
    lane_head = np.arange(LANES) // HEAD_DIM
    kdec_p = np.stack([kdec[2 * p + lane_head].T for p in range(PAIRS)])
    qdec_p = np.stack([qdec[2 * p + lane_head].T for p in range(PAIRS)])
    cdec_p = np.stack([np.broadcast_to(cdec[2 * p + lane_head][None, :], (LANES, LANES)) for p in range(PAIRS)])
    f = lambda a: jnp.asarray(a, F32)
    return f(dmask), f(kdec_p), f(qdec_p), f(cdec_p)


def _retention(qr, kr, vr, gate, g_pad, B):
    S = qr.shape[0]
    dmask, kdec, qdec, cdec = _ret_tables(B)
    blk = pl.BlockSpec((B, GROUP_W), lambda b: (b, 0))
    const = lambda a: pl.BlockSpec(a.shape, lambda b: (0,) * a.ndim)
    return pl.pallas_call(
        _ret_kernel,
        grid=(S // B,),
        in_specs=[blk, blk, blk, blk, const(dmask), const(kdec), const(qdec), const(cdec), const(g_pad)],
        out_specs=blk,
        out_shape=jax.ShapeDtypeStruct((S, GROUP_W), BF16),
        scratch_shapes=[pltpu.VMEM((PAIRS, LANES, LANES), F32)],
        compiler_params=_params("arbitrary"),
        name="retention",
    )(qr, kr, vr, gate, dmask, kdec, qdec, cdec, g_pad)


def _head_rms_out(acc0, acc1, g, m0):
    o = jnp.where(m0, acc0, acc1)
    ms = _per_head_mean(o * o, m0)
    return (o * lax.rsqrt(ms + EPS) * g).astype(BF16)


def _pair_rows(q, m0):
    zero = jnp.zeros_like(q)
    return jnp.concatenate([jnp.where(m0, q, zero), jnp.where(m0, zero, q)], axis=0)


def _rows2(a0, a1, tq):
    w = a0.shape[1]
    return jnp.concatenate([jnp.broadcast_to(a0, (tq, w)), jnp.broadcast_to(a1, (tq, w))], axis=0)


def _max_key_norm(k_ref, kmax_ref, tq, m0):
    def blk(j, mx):
        kk = k_ref[pl.ds(pl.multiple_of(j * tq, tq), tq), :].astype(F32)
        n2 = _per_head_mean(kk * kk, m0) * float(HEAD_DIM)
        return jnp.maximum(mx, jnp.max(n2, axis=0, keepdims=True))

    n2max = lax.fori_loop(0, k_ref.shape[0] // tq, blk, jnp.zeros((1, LANES), F32))
    kmax_ref[...] = jnp.sqrt(n2max)


def _score_bound(qm, kmax_ref, tq):
    qf = qm.astype(F32)
    qn = jnp.sqrt(jnp.sum(qf * qf, axis=1, keepdims=True))
    kn = _rows2(kmax_ref[:, 0:1], kmax_ref[:, HEAD_DIM:HEAD_DIM + 1], tq)
    return qn * kn * BOUND_SLACK


def _sb_kernel(q_ref, k_ref, v_ref, tri_ref, g_ref, o_ref, acc_ref, kmax_ref, *, tq):
    i = pl.program_id(1)
    m0, _ = _half_masks((tq, LANES))

    @pl.when(i == 0)
    def _():
        _max_key_norm(k_ref, kmax_ref, tq, m0)

    qm = _pair_rows(q_ref[...], m0)
    zbound = _score_bound(qm, kmax_ref, tq)
    row = lax.broadcasted_iota(jnp.int32, (2 * tq, tq), 0) & (tq - 1)
    col = lax.broadcasted_iota(jnp.int32, (2 * tq, tq), 1)
    acc_ref[...] = jnp.zeros_like(acc_ref)

    def kblock(ref, j):
        return ref[pl.ds(pl.multiple_of(jnp.maximum(j, 0) * tq, tq), tq), :]

    def local_sums(j, diag):
        z = _dot_nt(qm, kblock(k_ref, j))
        l1m = -(jnp.maximum(z, 0.0) + jnp.log2(1.0 + jnp.exp2(-jnp.abs(z))))
        if diag:
            l1m = jnp.where(col < row, l1m, 0.0)
        hi = l1m.astype(BF16)
        lo = (l1m - hi.astype(F32)).astype(BF16)
        tri = tri_ref[...]
        return z, _dot(hi, tri) + _dot(lo, tri)

    def accumulate(j, z, ltri, carry, diag):
        csum = ltri + jnp.where(j >= 0, carry, -jnp.inf)
        a = jnp.exp2(z + csum)
        if diag:
            a = jnp.where(col < row, a, 0.0)
        acc_ref[...] += _dot(a.astype(BF16), kblock(v_ref, j))
        return jnp.where(j >= 0, csum[:, 0:1], carry)

    def alive(carry):
        return jnp.max(carry + zbound) >= EXP2_ZERO_BELOW

    def body(st):
        j, carry, _ = st
        carry = accumulate(j, *local_sums(j, False), carry, False)
        return j - 1, carry, alive(carry)

    z_d, l_d = local_sums(i, True)
    z_p, l_p = local_sums(i - 1, False)
    carry = accumulate(i, z_d, l_d, jnp.zeros((2 * tq, 1), F32), True)
    carry = accumulate(i - 1, z_p, l_p, carry, False)
    lax.while_loop(lambda st: jnp.logical_and(st[0] >= 0, st[2]), body, (i - 2, carry, alive(carry)))
    o_ref[...] = _head_rms_out(acc_ref[:tq], acc_ref[tq:], g_ref[...], m0)


def _stick_breaking(q, k, v, g_pad, tq):
    S = q.shape[0]
    tri = jnp.tril(jnp.ones((tq, tq), F32)).astype(BF16)
    return pl.pallas_call(
        functools.partial(_sb_kernel, tq=tq),
        grid=(PAIRS, S // tq),
        in_specs=[pl.BlockSpec((tq, LANES), lambda p, i: (i, p)),
                  pl.BlockSpec((S, LANES), lambda p, i: (0, p)),
                  pl.BlockSpec((S, LANES), lambda p, i: (0, p)),
                  pl.BlockSpec((tq, tq), lambda p, i: (0, 0)),
                  pl.BlockSpec((1, LANES), lambda p, i: (0, p))],
        out_specs=pl.BlockSpec((tq, LANES), lambda p, i: (i, p)),
        out_shape=jax.ShapeDtypeStruct((S, GROUP_W), BF16),
        scratch_shapes=[pltpu.VMEM((2 * tq, LANES), F32), pltpu.VMEM((1, LANES), F32)],
        compiler_params=_params("arbitrary", "arbitrary"),
        name="stick_breaking",
    )(q, k, v, tri, g_pad)


def _fox_kernel(q_ref, k_ref, v_ref, f_ref, g_ref, o_ref, acc_ref, out_ref, kmax_ref, s_ref, *, n_heads, tk):
    p, i = pl.program_id(0), pl.program_id(1)
    tq = 2 * tk
    m0, m1 = _half_masks((tq, LANES))

    @pl.when(i == 0)
    def _():
        _max_key_norm(k_ref, kmax_ref, tk, _half_masks((tk, LANES))[0])

    row = lax.broadcasted_iota(jnp.int32, (tq, tk), 0)
    col = lax.broadcasted_iota(jnp.int32, (tq, tk), 1)
    mask_lo = (col <= row) | (row >= tk)
    mask_hi = col <= row - tk
    out_ref[...] = jnp.zeros_like(out_ref)

    def kblock(ref, j):
        return ref[pl.ds(pl.multiple_of(jnp.maximum(j, 0) * tk, tk), tk), :]

    def head(hh, m):
        q = q_ref[...]
        qm = jnp.where(m, q, jnp.zeros_like(q))
        qf = qm.astype(F32)
        kmax = kmax_ref[:, hh * HEAD_DIM:hh * HEAD_DIM + 1]
        sbound = jnp.sqrt(jnp.sum(qf * qf, axis=1, keepdims=True)) * kmax * BOUND_SLACK
        fref = f_ref[0, 2 * i, hh:hh + 1, 0:1]
        acc_ref[...] = jnp.zeros_like(acc_ref)

        def scores(j, slot):
            s_ref[slot] = _dot_nt(qm, kblock(k_ref, j))

        def step(j, slot, nxt, m_run, mask=None):
            scores(nxt, 1 - slot)
            vs = kblock(v_ref, j)
            vaug = jnp.where(_half_masks((tk, LANES))[hh], vs, jnp.ones_like(vs))
            bias = jnp.where(j >= 0, fref - f_ref[0, jnp.maximum(j, 0), hh:hh + 1, :], -jnp.inf)
            t = s_ref[slot] + bias
            if mask is not None:
                t = jnp.where(mask, t, -jnp.inf)
            m_new = jnp.maximum(m_run, jnp.max(t, axis=1, keepdims=True))
            pr = jnp.exp2(t - m_new).astype(BF16)
            acc_ref[...] = jnp.exp2(m_run - m_new) * acc_ref[...] + _dot(pr, vaug)
            return m_new

        def alive(j, m_run):
            fend = f_ref[0, jnp.maximum(j, 0), hh:hh + 1, tk - 1:tk]
            return jnp.max(sbound - m_run + (fref - fend)) >= EXP2_ZERO_BELOW

        def body(st):
            j, m_run, _ = st
            m_run = step(j, 1, j - 1, m_run)
            m_run = step(j - 1, 0, j - 2, m_run)
            return j - 2, m_run, alive(j - 2, m_run)

        scores(2 * i, 1)
        m_run = step(2 * i, 1, 2 * i + 1, jnp.full((tq, 1), -jnp.inf, F32), mask_lo)
        m_run = step(2 * i + 1, 0, 2 * i - 1, m_run, mask_hi)
        j0 = 2 * i - 1
        lax.while_loop(lambda st: jnp.logical_and(st[0] >= 0, st[2]), body, (j0, m_run, alive(j0, m_run)))
        acc = acc_ref[...]
        out_ref[hh] = acc / jnp.max(jnp.where(m, -jnp.inf, acc), axis=1, keepdims=True)

    head(0, m0)

    @pl.when(2 * p + 1 < n_heads)
    def _():
        head(1, m1)

    o_ref[...] = _head_rms_out(out_ref[0], out_ref[1], g_ref[...], m0)


def _fox(q, k, v, f_pairs, g_pad, tk):
    S = q.shape[0]
    tq = 2 * tk
    return pl.pallas_call(
        functools.partial(_fox_kernel, n_heads=N_FOX, tk=tk),
        grid=(PAIRS, S // tq),
        in_specs=[pl.BlockSpec((tq, LANES), lambda p, i: (i, p)),
                  pl.BlockSpec((S, LANES), lambda p, i: (0, p)),
                  pl.BlockSpec((S, LANES), lambda p, i: (0, p)),
                  pl.BlockSpec((1, S // tk, 2, tk), lambda p, i: (p, 0, 0, 0)),
                  pl.BlockSpec((1, LANES), lambda p, i: (0, p))],
        out_specs=pl.BlockSpec((tq, LANES), lambda p, i: (i, p)),
        out_shape=jax.ShapeDtypeStruct((S, GROUP_W), BF16),
        scratch_shapes=[pltpu.VMEM((tq, LANES), F32), pltpu.VMEM((2, tq, LANES), F32),
                        pltpu.VMEM((1, LANES), F32), pltpu.VMEM((2, tq, tk), F32)],
        compiler_params=_params("arbitrary", "arbitrary"),
        name="fox",
    )(q, k, v, f_pairs, g_pad)


def _out_router_kernel(x_ref, ret_ref, sb_ref, fx_ref, w_ref, g_ref, whi_ref, wlo_ref, b_ref,
                       x1_ref, x1t_ref, idx_ref, gate_ref):
    mix = _dot(ret_ref[...], w_ref[0]) + _dot(sb_ref[...], w_ref[1]) + _dot(fx_ref[...], w_ref[2])
    x1 = x_ref[...] + mix
    x1_ref[...] = x1
    _store_row_tiles(x1t_ref, x1, x1.shape[0])
    h = _rmsnorm(x1, g_ref[...])
    hi = h.astype(BF16)
    lo = (h - hi.astype(F32)).astype(BF16)
    whi = whi_ref[...]
    logits = _dot(hi, whi) + _dot(lo, whi) + _dot(hi, wlo_ref[...]) + b_ref[...]
    lane = lax.broadcasted_iota(jnp.int32, logits.shape, 1).astype(F32)
    neg = -jnp.inf
    big = float(LANES)

    def top(vals):
        v = jnp.max(vals, axis=1, keepdims=True)
        at = jnp.min(jnp.where(vals == v, lane, big), axis=1, keepdims=True)
        return v, at

    lg = jnp.where(lane < N_GROUPS, logits, neg)
    gmax, grp = top(lg)
    p_grp = 1.0 / jnp.sum(jnp.exp(lg - gmax), axis=1, keepdims=True)
    lo_lane = N_GROUPS + EXPERTS_PER_GROUP * grp
    le = jnp.where((lane >= lo_lane) & (lane < lo_lane + EXPERTS_PER_GROUP), logits, neg)
    v1, i1 = top(le)
    v2, i2 = top(jnp.where(lane == i1, neg, le))
    e21 = jnp.exp(v2 - v1)
    g1 = p_grp / (1.0 + e21)
    g2 = p_grp * e21 / (1.0 + e21)
    idx = jnp.where(lane == 0.0, i1, i2) - float(N_GROUPS)
    idx_ref[...] = jnp.where(lane < TOP_K, idx, 0.0).astype(jnp.int32)
    gate_ref[...] = jnp.where(lane == 0.0, g1, jnp.where(lane == 1.0, g2, 0.0))


def _out_router(x, o_ret, o_sb, o_fx, w_out3, g, w_hi, w_lo, b_rt, ts):
    S = x.shape[0]
    row = lambda w: pl.BlockSpec((ts, w), lambda i: (i, 0))
    const = lambda a: pl.BlockSpec(a.shape, lambda i: (0,) * a.ndim)
    return pl.pallas_call(
        _out_router_kernel,
        grid=(S // ts,),
        in_specs=[row(D_MODEL), row(GROUP_W), row(GROUP_W), row(GROUP_W), const(w_out3), const(g),
                  const(w_hi), const(w_lo), const(b_rt)],
        out_specs=[row(D_MODEL), pl.BlockSpec((ts * ROW_TILE, LANES), lambda i: (i, 0)), row(LANES), row(LANES)],
        out_shape=[jax.ShapeDtypeStruct((S, D_MODEL), F32), jax.ShapeDtypeStruct((S * ROW_TILE, LANES), F32),
                   jax.ShapeDtypeStruct((S, LANES), jnp.int32), jax.ShapeDtypeStruct((S, LANES), F32)],
        compiler_params=_params("arbitrary"),
        name="out_router",
    )(x, o_ret, o_sb, o_fx, w_out3, g, w_hi, w_lo, b_rt)


ROW_TILE = D_MODEL // LANES
GATHER_BUFS = 3


def _store_row_tiles(ref, val, n):
    for c in range(ROW_TILE):
        ref[pl.ds(c, n, stride=ROW_TILE), :] = val[:, c * LANES:(c + 1) * LANES]


def _load_row_tiles(ref, first, n):
    return jnp.concatenate([ref[pl.ds(first * ROW_TILE + c, n, stride=ROW_TILE), :] for c in range(ROW_TILE)], axis=1)


def _tile_gather(src_hbm, dst, sem, idx_ref, base, n):
    def one(r):
        src_row = pl.multiple_of(idx_ref[base + r] * ROW_TILE, ROW_TILE)
        pltpu.make_async_copy(src_hbm.at[pl.ds(src_row, ROW_TILE), :], dst.at[pl.ds(r * ROW_TILE, ROW_TILE), :],
                              sem).start()

    def start_loop():
        lax.fori_loop(0, n, lambda r, c: (one(r), c)[1], 0, unroll=8)

    def start_unrolled(part=0, parts=1):
        for r in range(part * n // parts, (part + 1) * n // parts):
            one(r)

    def wait():
        pltpu.make_async_copy(src_hbm.at[pl.ds(0, n * ROW_TILE), :], dst, sem).wait()

    return start_loop, start_unrolled, wait


def _expert_kernel(be_ref, tok_ref, nused_ref, x_hbm, g_ref, wg_ref, wu_ref, wd_ref, y_ref,
                   xbuf, wg_bf, wu_bf, wd_bf, sem, *, rb):
    b = pl.program_id(0)
    nused = nused_ref[0]
    slot = b % GATHER_BUFS

    def gather(blk, s):
        return _tile_gather(x_hbm, xbuf.at[s], sem.at[s], tok_ref, jnp.minimum(blk, nused - 1) * rb, rb)

    @pl.when(b == 0)
    def _():
        for s in range(GATHER_BUFS - 1):
            gather(s, s)[0]()

    @pl.when(b < nused)
    def _():
        @pl.when(jnp.logical_or(b == 0, be_ref[b] != be_ref[jnp.maximum(b - 1, 0)]))
        def _():
            wg_bf[...] = wg_ref[0].astype(BF16)
            wu_bf[...] = wu_ref[0].astype(BF16)
            wd_bf[...] = wd_ref[0].astype(BF16)

        gather(b, slot)[2]()
        ahead = gather(b + GATHER_BUFS - 1, (b + GATHER_BUFS - 1) % GATHER_BUFS)[1]
        h = _rmsnorm(_load_row_tiles(xbuf.at[slot], 0, rb), g_ref[...]).astype(BF16)
        ahead(0, 4)
        gt = _dot(h, wg_bf[...])
        ahead(1, 4)
        up = _dot(h, wu_bf[...])
        ahead(2, 4)
        hid = gt * (1.0 / (1.0 + jnp.exp(-gt))) * up
        y = _dot(hid.astype(BF16), wd_bf[...])
        ahead(3, 4)
        _store_row_tiles(y_ref, y, rb)

    @pl.when(b == nused)
    def _():
        for s in range(GATHER_BUFS - 1):
            gather(b, (b + s) % GATHER_BUFS)[2]()

    @pl.when(b >= nused)
    def _():
        y_ref[...] = jnp.zeros_like(y_ref)


def _experts(blk_expert, tok_slot, nused, x1, g, w_gate, w_up, w_down, layer, rb):
    P = tok_slot.shape[0]
    nblk = P // rb
    live = lambda b, nu: jnp.minimum(b, nu[0] - 1)
    wsel = lambda b, be, tk, nu: (layer * N_EXPERTS + be[live(b, nu)], 0, 0)
    grid_spec = pltpu.PrefetchScalarGridSpec(
        num_scalar_prefetch=3,
        grid=(nblk,),
        in_specs=[pl.BlockSpec(memory_space=pl.ANY),
                  pl.BlockSpec((1, D_MODEL), lambda b, be, tk, nu: (0, 0)),
                  pl.BlockSpec((1, D_MODEL, D_EXPERT), wsel),
                  pl.BlockSpec((1, D_MODEL, D_EXPERT), wsel),
                  pl.BlockSpec((1, D_EXPERT, D_MODEL), wsel)],
        out_specs=pl.BlockSpec((rb * ROW_TILE, LANES), lambda b, be, tk, nu: (b, 0)),
        scratch_shapes=[pltpu.VMEM((GATHER_BUFS, rb * ROW_TILE, LANES), F32),
                        pltpu.VMEM((D_MODEL, D_EXPERT), BF16), pltpu.VMEM((D_MODEL, D_EXPERT), BF16),
                        pltpu.VMEM((D_EXPERT, D_MODEL), BF16), pltpu.SemaphoreType.DMA((GATHER_BUFS,))],
    )
    return pl.pallas_call(
        functools.partial(_expert_kernel, rb=rb),
        grid_spec=grid_spec,
        out_shape=jax.ShapeDtypeStruct((P * ROW_TILE, LANES), F32),
        compiler_params=_params("arbitrary"),
        name="experts",
    )(blk_expert, tok_slot, nused, x1, g, w_gate, w_up, w_down)


def _combine_kernel(dest_ref, x_ref, gate_ref, y_hbm, gf_ref, o_ref, ybuf, sem, *, tb, final):
    b = pl.program_id(0)
    nb = pl.num_programs(0)
    slot = b % 2

    def gather(blk, s):
        return _tile_gather(y_hbm, ybuf.at[s], sem.at[s], dest_ref, blk * (TOP_K * tb), TOP_K * tb)

    @pl.when(b == 0)
    def _():
        gather(0, 0)[0]()

    @pl.when(b + 1 < nb)
    def _():
        gather(b + 1, 1 - slot)[0]()

    gather(b, slot)[2]()
    gate = gate_ref[...]
    y0 = _load_row_tiles(ybuf.at[slot], 0, tb)
    y1 = _load_row_tiles(ybuf.at[slot], tb, tb)
    out = x_ref[...] + (gate[:, 0:1] * y0 + gate[:, 1:2] * y1)
    if final:
        out = _rmsnorm(out, gf_ref[...])
    o_ref[...] = out


def _combine(dest_blocked, x1, gates, ybuf, g_final, tb, final):
    S = x1.shape[0]
    grid_spec = pltpu.PrefetchScalarGridSpec(
        num_scalar_prefetch=1,
        grid=(S // tb,),
        in_specs=[pl.BlockSpec((tb, D_MODEL), lambda b, d: (b, 0)),
                  pl.BlockSpec((tb, LANES), lambda b, d: (b, 0)),
                  pl.BlockSpec(memory_space=pl.ANY),
                  pl.BlockSpec((1, D_MODEL), lambda b, d: (0, 0))],
        out_specs=pl.BlockSpec((tb, D_MODEL), lambda b, d: (b, 0)),
        scratch_shapes=[pltpu.VMEM((2, TOP_K * tb * ROW_TILE, LANES), F32), pltpu.SemaphoreType.DMA((2,))],
    )
    return pl.pallas_call(
        functools.partial(_combine_kernel, tb=tb, final=final),
        grid_spec=grid_spec,
        out_shape=jax.ShapeDtypeStruct((S, D_MODEL), F32),
        compiler_params=_params("arbitrary"),
        name="combine",
    )(dest_blocked, x1, gates, ybuf, g_final)


def _pad_cols(w, width):
    return jnp.pad(w, ((0, 0), (0, width - w.shape[1])))


def _proj_weights(w_in):
    w_ret, w_sb, w_fox = N_RET * HEAD_DIM, N_SB * HEAD_DIM, N_FOX * HEAD_DIM
    sizes = [w_ret] * 4 + [w_sb] * 3 + [w_fox] * 3 + [N_FOX]
    offs = np.concatenate([[0], np.cumsum(sizes)])
    q_r, k_r, v_r, gate_r, q_s, k_s, v_s, q_f, k_f, v_f, f_l = [w_in[:, offs[n]:offs[n + 1]] for n in range(11)]
    q_r = q_r * Q_SCALE
    segs = [q_r, k_r, v_r, gate_r,
            q_s * (Q_SCALE * LOG2E), k_s, v_s, q_f * (Q_SCALE * LOG2E), k_f, v_f]
    cols = [_pad_cols(s, GROUP_W) for s in segs] + [_pad_cols(f_l, LANES)]
    return jnp.concatenate(cols, axis=1).astype(BF16)


def _rope_tables(S):
    inv = 1.0 / (ROPE_BASE ** (jnp.arange(HALF, dtype=F32) / HALF))
    ang = jnp.arange(S).astype(F32)[:, None] * inv[None, :]
    cos, sin = jnp.cos(ang), jnp.sin(ang)
    return jnp.concatenate([cos] * 4, axis=1), jnp.concatenate([sin] * 4, axis=1)


def _slot_tokens_kernel(dest_ref, tok_ref):
    def zero(s, c):
        tok_ref[s] = 0
        return c

    def put(a, c):
        tok_ref[dest_ref[a]] = lax.shift_right_logical(a, 1)
        return c

    lax.fori_loop(0, tok_ref.shape[0], zero, 0, unroll=8)
    lax.fori_loop(0, dest_ref.shape[0], put, 0, unroll=8)


def _slot_tokens(dest, n_slots):
    return pl.pallas_call(
        _slot_tokens_kernel,
        in_specs=[pl.BlockSpec(memory_space=pltpu.SMEM)],
        out_specs=pl.BlockSpec(memory_space=pltpu.SMEM),
        out_shape=jax.ShapeDtypeStruct((n_slots,), jnp.int32),
        name="slot_tokens",
    )(dest)


def _routing_tables(idx, rb):
    T = idx.shape[0]
    A = T * TOP_K
    P = A + N_EXPERTS * rb
    e = idx[:, :TOP_K].reshape(A)
    onehot = (e[:, None] == jnp.arange(N_EXPERTS, dtype=jnp.int32)[None, :]).astype(jnp.int32)
    cs = jnp.cumsum(onehot, axis=0)
    rank = jnp.sum(cs * onehot, axis=1) - 1
    counts = cs[-1]
    padded = ((counts + rb - 1) // rb) * rb
    pend = jnp.cumsum(padded)
    pstart = pend - padded
    dest = (jnp.sum(onehot * pstart[None, :], axis=1) + rank).astype(jnp.int32)
    tok_slot = _slot_tokens(dest, P)
    blk_start = jnp.arange(P // rb, dtype=jnp.int32) * rb
    blk_expert = jnp.minimum(jnp.sum((pend[None, :] <= blk_start[:, None]).astype(jnp.int32), axis=1),
                             N_EXPERTS - 1).astype(jnp.int32)
    nused = (pend[-1] // rb).astype(jnp.int32).reshape(1)
    return dest, tok_slot, blk_expert, nused


def kernel(x, norm_mix, w_in, b_forget, g_ret, g_sb, g_fox, w_out, norm_ffn, w_group, b_group,
           w_router, b_router, w_gate, w_up, w_down, norm_final):
    S = x.shape[1]
    ts = min(512, S)
    tq = min(256, S)
    rb = 256
    tb = min(256, S)
    depth = w_in.shape[0]
    xs = x.reshape(S, D_MODEL)
    cos_t, sin_t = _rope_tables(S)
    row1 = lambda v, width: jnp.pad(v, (0, width - v.shape[0])).reshape(1, width)
    nq = S // tq

    for l in range(depth):
        w_all = _proj_weights(w_in[l])
        qr, kr, vr, gate, qs, ks, vs, qf, kf, vf, fcum = _proj(
            xs, norm_mix[l].reshape(1, D_MODEL), w_all, cos_t, sin_t, row1(b_forget[l], LANES), ts)
        o_ret = _retention(qr, kr, vr, gate, row1(g_ret[l], GROUP_W), tq)
        o_sb = _stick_breaking(qs, ks, vs, row1(g_sb[l], GROUP_W), tq)
        f_pairs = (fcum[:, :2 * PAIRS] * LOG2E).T.reshape(PAIRS, 2, nq, tq).transpose(0, 2, 1, 3)
        o_fx = _fox(qf, kf, vf, f_pairs, row1(g_fox[l], GROUP_W), tq)

        w_o = w_out[l]
        w_ret, w_sb = N_RET * HEAD_DIM, N_SB * HEAD_DIM
        pad_rows = lambda w: jnp.pad(w, ((0, GROUP_W - w.shape[0]), (0, 0)))
        w_out3 = jnp.stack([pad_rows(w_o[:w_ret]), pad_rows(w_o[w_ret:w_ret + w_sb]),
                            pad_rows(w_o[w_ret + w_sb:])]).astype(BF16)
        w_rt = _pad_cols(jnp.concatenate([w_group[l], w_router[l]], axis=1), LANES)
        w_hi = w_rt.astype(BF16)
        w_lo = (w_rt - w_hi.astype(F32)).astype(BF16)
        b_rt = row1(jnp.concatenate([b_group[l], b_router[l]]), LANES)
        g_ffn = norm_ffn[l].reshape(1, D_MODEL)
        x1, x1_tiles, ridx, rgate = _out_router(xs, o_ret, o_sb, o_fx, w_out3, g_ffn, w_hi, w_lo, b_rt, ts)

        dest, tok_slot, blk_expert, nused = _routing_tables(ridx, rb)
        flat = lambda w: w.reshape((depth * N_EXPERTS,) + w.shape[2:])
        ybuf = _experts(blk_expert, tok_slot, nused, x1_tiles, g_ffn, flat(w_gate), flat(w_up), flat(w_down), l, rb)
        dest_blocked = dest.reshape(S // tb, tb, TOP_K).transpose(0, 2, 1).reshape(-1)
        xs = _combine(dest_blocked, x1, rgate, ybuf, norm_final.reshape(1, D_MODEL), tb, l == depth - 1)
    return xs.reshape(x.shape)
```

```python
import functools

import jax
import jax.numpy as jnp
import numpy as np
from jax import lax
from jax.experimental import pallas as pl
from jax.experimental.pallas import tpu as pltpu

F32 = jnp.float32
BF16 = jnp.bfloat16

D_MODEL = 1024
HEAD_DIM = 64
HALF = HEAD_DIM // 2
N_RET, N_SB, N_FOX = 6, 5, 5
PAIRS = 3
GROUP_W = PAIRS * 128
CHUNK = 64
ROPE_BASE = 10000.0
EPS = 1e-6
N_GROUPS, EXPERTS_PER_GROUP = 4, 8
N_EXPERTS = N_GROUPS * EXPERTS_PER_GROUP
TOP_K = 2
D_EXPERT = 512
Q_SCALE = HEAD_DIM ** -0.5

LOG2E = 1.4426950408889634
EXP2_ZERO_BELOW = -159.0
BOUND_SLACK = 1.0 + 2.0 ** -10

LANES = 128
VMEM_LIMIT = 56 * 1024 * 1024

SEG_QR, SEG_KR, SEG_VR, SEG_GATE = 0, 1, 2, 3
SEG_QS, SEG_KS, SEG_VS, SEG_QF, SEG_KF, SEG_VF = 4, 5, 6, 7, 8, 9
N_SEG = 10
PROJ_W = N_SEG * GROUP_W + LANES


def _params(*sem):
    return pltpu.CompilerParams(dimension_semantics=sem, vmem_limit_bytes=VMEM_LIMIT)


def _rmsnorm(x, g):
    return x * lax.rsqrt(jnp.mean(x * x, axis=-1, keepdims=True) + EPS) * g


def _split3(x):
    hi = x.astype(BF16)
    r = x - hi.astype(F32)
    mid = r.astype(BF16)
    lo = (r - mid.astype(F32)).astype(BF16)
    return hi, mid, lo


def _dot(a, b):
    return jnp.dot(a, b, preferred_element_type=F32)


def _dot_nt(a, b):
    return lax.dot_general(a, b, (((1,), (1,)), ((), ())), preferred_element_type=F32)


def _proj_kernel(x_ref, g_ref, w_ref, cos_ref, sin_ref, bf_ref, tri_ref,
                 qr_ref, kr_ref, vr_ref, gate_ref, qs_ref, ks_ref, vs_ref, qf_ref, kf_ref, vf_ref,
                 fcum_ref, carry_ref):
    @pl.when(pl.program_id(0) == 0)
    def _():
        carry_ref[...] = jnp.zeros_like(carry_ref)

    h = _rmsnorm(x_ref[...], g_ref[...]).astype(BF16)

    def seg(k, width=GROUP_W):
        return _dot(h, w_ref[:, k * GROUP_W:k * GROUP_W + width])

    cos, sin = cos_ref[...], sin_ref[...]
    first_half = (lax.broadcasted_iota(jnp.int32, cos.shape, 1) & (HEAD_DIM - 1)) < HALF

    def rotary(x):
        out = []
        for p in range(PAIRS):
            xp = x[:, p * LANES:(p + 1) * LANES]
            swapped = jnp.where(first_half, -pltpu.roll(xp, LANES - HALF, 1), pltpu.roll(xp, HALF, 1))
            out.append(xp * cos + swapped * sin)
        return jnp.concatenate(out, axis=1).astype(BF16)

    qr_ref[...] = rotary(seg(SEG_QR))
    kr_ref[...] = rotary(seg(SEG_KR))
    vr_ref[...] = seg(SEG_VR).astype(BF16)
    gate = seg(SEG_GATE)
    gate_ref[...] = gate * (1.0 / (1.0 + jnp.exp(-gate)))
    qs_ref[...] = seg(SEG_QS).astype(BF16)
    ks_ref[...] = seg(SEG_KS).astype(BF16)
    vs_ref[...] = seg(SEG_VS).astype(BF16)
    qf_ref[...] = seg(SEG_QF).astype(BF16)
    kf_ref[...] = seg(SEG_KF).astype(BF16)
    vf_ref[...] = seg(SEG_VF).astype(BF16)

    fl = seg(N_SEG, LANES) + bf_ref[...]
    logf = jnp.minimum(fl, 0.0) - jnp.log1p(jnp.exp(-jnp.abs(fl)))
    hi, mid, lo = _split3(logf)
    tri = tri_ref[...]
    fc = _dot(tri, hi) + _dot(tri, mid) + _dot(tri, lo) + carry_ref[...]
    fcum_ref[...] = fc
    carry_ref[...] = fc[-1:, :]


def _proj(x, g, w_all, cos_t, sin_t, bf_pad, ts):
    S = x.shape[0]
    tri = jnp.tril(jnp.ones((ts, ts), F32)).astype(BF16)
    row = lambda w: pl.BlockSpec((ts, w), lambda i: (i, 0))
    const = lambda a: pl.BlockSpec(a.shape, lambda i: (0,) * a.ndim)
    bshape = jax.ShapeDtypeStruct((S, GROUP_W), BF16)
    outs = [bshape, bshape, bshape, jax.ShapeDtypeStruct((S, GROUP_W), F32)] + [bshape] * 6 + [
        jax.ShapeDtypeStruct((S, LANES), F32)]
    return pl.pallas_call(
        _proj_kernel,
        grid=(S // ts,),
        in_specs=[row(D_MODEL), const(g), const(w_all), row(LANES), row(LANES), const(bf_pad), const(tri)],
        out_specs=[row(GROUP_W)] * 10 + [row(LANES)],
        out_shape=outs,
        scratch_shapes=[pltpu.VMEM((1, LANES), F32)],
        compiler_params=_params("arbitrary"),
        name="proj",
    )(x, g, w_all, cos_t, sin_t, bf_pad, tri)


def _half_masks(shape):
    lane = lax.broadcasted_iota(jnp.int32, shape, len(shape) - 1)
    return lane < HEAD_DIM, lane >= HEAD_DIM


def _per_head_mean(x, m0):
    s0 = jnp.sum(jnp.where(m0, x, 0.0), axis=1, keepdims=True)
    s1 = jnp.sum(jnp.where(m0, 0.0, x), axis=1, keepdims=True)
    return jnp.where(m0, s0, s1) * (1.0 / HEAD_DIM)


def _ret_kernel(q_ref, k_ref, v_ref, gate_ref, dmask_ref, kdec_ref, qdec_ref, cdec_ref, g_ref,
                o_ref, state_ref):
    @pl.when(pl.program_id(0) == 0)
    def _():
        state_ref[...] = jnp.zeros_like(state_ref)

    m0, m1 = _half_masks((q_ref.shape[0], LANES))
    for p in range(PAIRS):
        lanes = slice(p * LANES, (p + 1) * LANES)
        q, k, v = q_ref[:, lanes], k_ref[:, lanes], v_ref[:, lanes]
        state = state_ref[p]
        qd = (q.astype(F32) * qdec_ref[p]).astype(BF16)
        o = _dot(qd, state.astype(BF16))
        for hh, m in enumerate((m0, m1)):
            s = _dot_nt(jnp.where(m, q, jnp.zeros_like(q)), k)
            oh = _dot((s * dmask_ref[2 * p + hh]).astype(BF16), v)
            o = o + jnp.where(m, oh, 0.0)
        kdt = (k.astype(F32) * kdec_ref[p]).T.astype(BF16)
        u = _dot(kdt, v)
        r = lax.broadcasted_iota(jnp.int32, u.shape, 0) < HEAD_DIM
        c = lax.broadcasted_iota(jnp.int32, u.shape, 1) < HEAD_DIM
        state_ref[p] = state * cdec_ref[p] + jnp.where(r == c, u, 0.0)
        cen = o - _per_head_mean(o, m0)
        var = _per_head_mean(cen * cen, m0)
        o_ref[:, lanes] = (cen * lax.rsqrt(var + EPS) * g_ref[:, lanes] * gate_ref[:, lanes]).astype(BF16)


def _ret_tables(B):
    hs = np.arange(N_RET, dtype=np.float64)
    lg = np.log(1.0 - 2.0 ** (-5.0 - hs))
    idx = np.arange(B, dtype=np.float64)
    diff = idx[:, None] - idx[None, :]
    ch = np.arange(B) // CHUNK
    same = ch[:, None] == ch[None, :]
    earlier = ch[None, :] < ch[:, None]
    expo = np.where(same, np.abs(diff), diff)
    dmask = np.where(same | earlier, np.exp(lg[:, None, None] * expo[None]), 0.0)
    kdec = np.exp(lg[:, None] * (B - 1.0 - idx)[None])
    qdec = np.exp(lg[:, None] * (idx + 1.0)[None])
    cdec = np.exp(lg * B)
    lane_head = np.arange(LANES) // HEAD_DIM
    kdec_p = np.stack([kdec[2 * p + lane_head].T for p in range(PAIRS)])
    qdec_p = np.stack([qdec[2 * p + lane_head].T for p in range(PAIRS)])
    cdec_p = np.stack([np.broadcast_to(cdec[2 * p + lane_head][None, :], (LANES, LANES)) for p in range(PAIRS)])
    f = lambda a: jnp.asarray(a, F32)
    return f(dmask), f(kdec_p), f(qdec_p), f(cdec_p)


def _retention(qr, kr, vr, gate, g_pad, B):
    S = qr.shape[0]
    dmask, kdec, qdec, cdec = _ret_tables(B)
    blk = pl.BlockSpec((B, GROUP_W), lambda b: (b, 0))
    const = lambda a: pl.BlockSpec(a.shape, lambda b: (0,) * a.ndim)
    return pl.pallas_call(
        _ret_kernel,
        grid=(S // B,),
        in_specs=[blk, blk, blk, blk, const(dmask), const(kdec), const(qdec), const(cdec), const(g_pad)],
        out_specs=blk,
        out_shape=jax.ShapeDtypeStruct((S, GROUP_W), BF16),
        scratch_shapes=[pltpu.VMEM((PAIRS, LANES, LANES), F32)],
        compiler_params=_params("arbitrary"),
        name="retention",
    )(qr, kr, vr, gate, dmask, kdec, qdec, cdec, g_pad)


def _head_rms_out(acc0, acc1, g, m0):
    o = jnp.where(m0, acc0, acc1)
    ms = _per_head_mean(o * o, m0)
    return (o * lax.rsqrt(ms + EPS) * g).astype(BF16)


def _pair_rows(q, m0):
    zero = jnp.zeros_like(q)
    return jnp.concatenate([jnp.where(m0, q, zero), jnp.where(m0, zero, q)], axis=0)


def _rows2(a0, a1, tq):
    w = a0.shape[1]
    return jnp.concatenate([jnp.broadcast_to(a0, (tq, w)), jnp.broadcast_to(a1, (tq, w))], axis=0)


def _max_key_norm(k_ref, kmax_ref, tq, m0):
    def blk(j, mx):
        kk = k_ref[pl.ds(pl.multiple_of(j * tq, tq), tq), :].astype(F32)
        n2 = _per_head_mean(kk * kk, m0) * float(HEAD_DIM)
        return jnp.maximum(mx, jnp.max(n2, axis=0, keepdims=True))

    n2max = lax.fori_loop(0, k_ref.shape[0] // tq, blk, jnp.zeros((1, LANES), F32))
    kmax_ref[...] = jnp.sqrt(n2max)


def _score_bound(qm, kmax_ref, tq):
    qf = qm.astype(F32)
    qn = jnp.sqrt(jnp.sum(qf * qf, axis=1, keepdims=True))
    kn = _rows2(kmax_ref[:, 0:1], kmax_ref[:, HEAD_DIM:HEAD_DIM + 1], tq)
    return qn * kn * BOUND_SLACK


def _sb_kernel(q_ref, k_ref, v_ref, tri_ref, g_ref, o_ref, acc_ref, kmax_ref, *, tq):
    i = pl.program_id(1)
    m0, _ = _half_masks((tq, LANES))

    @pl.when(i == 0)
    def _():
        _max_key_norm(k_ref, kmax_ref, tq, m0)

    qm = _pair_rows(q_ref[...], m0)
    zbound = _score_bound(qm, kmax_ref, tq)
    row = lax.broadcasted_iota(jnp.int32, (2 * tq, tq), 0) & (tq - 1)
    col = lax.broadcasted_iota(jnp.int32, (2 * tq, tq), 1)
    acc_ref[...] = jnp.zeros_like(acc_ref)

    def kblock(ref, j):
        return ref[pl.ds(pl.multiple_of(jnp.maximum(j, 0) * tq, tq), tq), :]

    def local_sums(j, diag):
        z = _dot_nt(qm, kblock(k_ref, j))
        l1m = -(jnp.maximum(z, 0.0) + jnp.log2(1.0 + jnp.exp2(-jnp.abs(z))))
        if diag:
            l1m = jnp.where(col < row, l1m, 0.0)
        hi = l1m.astype(BF16)
        lo = (l1m - hi.astype(F32)).astype(BF16)
        tri = tri_ref[...]
        return z, _dot(hi, tri) + _dot(lo, tri)

    def accumulate(j, z, ltri, carry, diag):
        csum = ltri + jnp.where(j >= 0, carry, -jnp.inf)
        a = jnp.exp2(z + csum)
        if diag:
            a = jnp.where(col < row, a, 0.0)
        acc_ref[...] += _dot(a.astype(BF16), kblock(v_ref, j))
        return jnp.where(j >= 0, csum[:, 0:1], carry)

    def alive(carry):
        return jnp.max(carry + zbound) >= EXP2_ZERO_BELOW

    def body(st):
        j, carry, _ = st
        carry = accumulate(j, *local_sums(j, False), carry, False)
        return j - 1, carry, alive(carry)

    z_d, l_d = local_sums(i, True)
    z_p, l_p = local_sums(i - 1, False)
    carry = accumulate(i, z_d, l_d, jnp.zeros((2 * tq, 1), F32), True)
    carry = accumulate(i - 1, z_p, l_p, carry, False)
    lax.while_loop(lambda st: jnp.logical_and(st[0] >= 0, st[2]), body, (i - 2, carry, alive(carry)))
    o_ref[...] = _head_rms_out(acc_ref[:tq], acc_ref[tq:], g_ref[...], m0)


def _stick_breaking(q, k, v, g_pad, tq):
    S = q.shape[0]
    tri = jnp.tril(jnp.ones((tq, tq), F32)).astype(BF16)
    return pl.pallas_call(
        functools.partial(_sb_kernel, tq=tq),
        grid=(PAIRS, S // tq),
        in_specs=[pl.BlockSpec((tq, LANES), lambda p, i: (i, p)),
                  pl.BlockSpec((S, LANES), lambda p, i: (0, p)),
                  pl.BlockSpec((S, LANES), lambda p, i: (0, p)),
                  pl.BlockSpec((tq, tq), lambda p, i: (0, 0)),
                  pl.BlockSpec((1, LANES), lambda p, i: (0, p))],
        out_specs=pl.BlockSpec((tq, LANES), lambda p, i: (i, p)),
        out_shape=jax.ShapeDtypeStruct((S, GROUP_W), BF16),
        scratch_shapes=[pltpu.VMEM((2 * tq, LANES), F32), pltpu.VMEM((1, LANES), F32)],
        compiler_params=_params("arbitrary", "arbitrary"),
        name="stick_breaking",
    )(q, k, v, tri, g_pad)


def _fox_kernel(q_ref, k_ref, v_ref, f_ref, g_ref, o_ref, acc_ref, out_ref, kmax_ref, s_ref, *, n_heads, tk):
    p, i = pl.program_id(0), pl.program_id(1)
    tq = 2 * tk
    m0, m1 = _half_masks((tq, LANES))

    @pl.when(i == 0)
    def _():
        _max_key_norm(k_ref, kmax_ref, tk, _half_masks((tk, LANES))[0])

    row = lax.broadcasted_iota(jnp.int32, (tq, tk), 0)
    col = lax.broadcasted_iota(jnp.int32, (tq, tk), 1)
    mask_lo = (col <= row) | (row >= tk)
    mask_hi = col <= row - tk
    out_ref[...] = jnp.zeros_like(out_ref)

    def kblock(ref, j):
        return ref[pl.ds(pl.multiple_of(jnp.maximum(j, 0) * tk, tk), tk), :]

    def head(hh, m):
        q = q_ref[...]
        qm = jnp.where(m, q, jnp.zeros_like(q))
        qf = qm.astype(F32)
        kmax = kmax_ref[:, hh * HEAD_DIM:hh * HEAD_DIM + 1]
        sbound = jnp.sqrt(jnp.sum(qf * qf, axis=1, keepdims=True)) * kmax * BOUND_SLACK
        fref = f_ref[0, 2 * i, hh:hh + 1, 0:1]
        acc_ref[...] = jnp.zeros_like(acc_ref)

        def scores(j, slot):
            s_ref[slot] = _dot_nt(qm, kblock(k_ref, j))

        def step(j, slot, nxt, m_run, mask=None):
            scores(nxt, 1 - slot)
            vs = kblock(v_ref, j)
            vaug = jnp.where(_half_masks((tk, LANES))[hh], vs, jnp.ones_like(vs))
            bias = jnp.where(j >= 0, fref - f_ref[0, jnp.maximum(j, 0), hh:hh + 1, :], -jnp.inf)
            t = s_ref[slot] + bias
            if mask is not None:
                t = jnp.where(mask, t, -jnp.inf)
            m_new = jnp.maximum(m_run, jnp.max(t, axis=1, keepdims=True))
            pr = jnp.exp2(t - m_new).astype(BF16)
            acc_ref[...] = jnp.exp2(m_run - m_new) * acc_ref[...] + _dot(pr, vaug)
            return m_new

        def alive(j, m_run):
            fend = f_ref[0, jnp.maximum(j, 0), hh:hh + 1, tk - 1:tk]
            return jnp.max(sbound - m_run + (fref - fend)) >= EXP2_ZERO_BELOW

        def body(st):
            j, m_run, _ = st
            m_run = step(j, 1, j - 1, m_run)
            m_run = step(j - 1, 0, j - 2, m_run)
            return j - 2, m_run, alive(j - 2, m_run)

        scores(2 * i, 1)
        m_run = step(2 * i, 1, 2 * i + 1, jnp.full((tq, 1), -jnp.inf, F32), mask_lo)
        m_run = step(2 * i + 1, 0, 2 * i - 1, m_run, mask_hi)
        j0 = 2 * i - 1
        lax.while_loop(lambda st: jnp.logical_and(st[0] >= 0, st[2]), body, (j0, m_run, alive(j0, m_run)))
        acc = acc_ref[...]
        out_ref[hh] = acc / jnp.max(jnp.where(m, -jnp.inf, acc), axis=1, keepdims=True)

    head(0, m0)

    @pl.when(2 * p + 1 < n_heads)
    def _():
        head(1, m1)

    o_ref[...] = _head_rms_out(out_ref[0], out_ref[1], g_ref[...], m0)


def _fox(q, k, v, f_pairs, g_pad, tk):
    S = q.shape[0]
    tq = 2 * tk
    return pl.pallas_call(
        functools.partial(_fox_kernel, n_heads=N_FOX, tk=tk),
        grid=(PAIRS, S // tq),
        in_specs=[pl.BlockSpec((tq, LANES), lambda p, i: (i, p)),
                  pl.BlockSpec((S, LANES), lambda p, i: (0, p)),
                  pl.BlockSpec((S, LANES), lambda p, i: (0, p)),
                  pl.BlockSpec((1, S // tk, 2, tk), lambda p, i: (p, 0, 0, 0)),
                  pl.BlockSpec((1, LANES), lambda p, i: (0, p))],
        out_specs=pl.BlockSpec((tq, LANES), lambda p, i: (i, p)),
        out_shape=jax.ShapeDtypeStruct((S, GROUP_W), BF16),
        scratch_shapes=[pltpu.VMEM((tq, LANES), F32), pltpu.VMEM((2, tq, LANES), F32),
                        pltpu.VMEM((1, LANES), F32), pltpu.VMEM((2, tq, tk), F32)],
        compiler_params=_params("arbitrary", "arbitrary"),
        name="fox",
    )(q, k, v, f_pairs, g_pad)


def _out_router_kernel(x_ref, ret_ref, sb_ref, fx_ref, w_ref, g_ref, whi_ref, wlo_ref, b_ref,
                       x1_ref, x1t_ref, idx_ref, gate_ref):
    mix = _dot(ret_ref[...], w_ref[0]) + _dot(sb_ref[...], w_ref[1]) + _dot(fx_ref[...], w_ref[2])
    x1 = x_ref[...] + mix
    x1_ref[...] = x1
    _store_row_tiles(x1t_ref, x1, x1.shape[0])
    h = _rmsnorm(x1, g_ref[...])
    hi = h.astype(BF16)
    lo = (h - hi.astype(F32)).astype(BF16)
    whi = whi_ref[...]
    logits = _dot(hi, whi) + _dot(lo, whi) + _dot(hi, wlo_ref[...]) + b_ref[...]
    lane = lax.broadcasted_iota(jnp.int32, logits.shape, 1).astype(F32)
    neg = -jnp.inf
    big = float(LANES)

    def top(vals):
        v = jnp.max(vals, axis=1, keepdims=True)
        at = jnp.min(jnp.where(vals == v, lane, big), axis=1, keepdims=True)
        return v, at

    lg = jnp.where(lane < N_GROUPS, logits, neg)
    gmax, grp = top(lg)
    p_grp = 1.0 / jnp.sum(jnp.exp(lg - gmax), axis=1, keepdims=True)
    lo_lane = N_GROUPS + EXPERTS_PER_GROUP * grp
    le = jnp.where((lane >= lo_lane) & (lane < lo_lane + EXPERTS_PER_GROUP), logits, neg)
    v1, i1 = top(le)
    v2, i2 = top(jnp.where(lane == i1, neg, le))
    e21 = jnp.exp(v2 - v1)
    g1 = p_grp / (1.0 + e21)
    g2 = p_grp * e21 / (1.0 + e21)
    idx = jnp.where(lane == 0.0, i1, i2) - float(N_GROUPS)
    idx_ref[...] = jnp.where(lane < TOP_K, idx, 0.0).astype(jnp.int32)
    gate_ref[...] = jnp.where(lane == 0.0, g1, jnp.where(lane == 1.0, g2, 0.0))


def _out_router(x, o_ret, o_sb, o_fx, w_out3, g, w_hi, w_lo, b_rt, ts):
    S = x.shape[0]
    row = lambda w: pl.BlockSpec((ts, w), lambda i: (i, 0))
    const = lambda a: pl.BlockSpec(a.shape, lambda i: (0,) * a.ndim)
    return pl.pallas_call(
        _out_router_kernel,
        grid=(S // ts,),
        in_specs=[row(D_MODEL), row(GROUP_W), row(GROUP_W), row(GROUP_W), const(w_out3), const(g),
                  const(w_hi), const(w_lo), const(b_rt)],
        out_specs=[row(D_MODEL), pl.BlockSpec((ts * ROW_TILE, LANES), lambda i: (i, 0)), row(LANES), row(LANES)],
        out_shape=[jax.ShapeDtypeStruct((S, D_MODEL), F32), jax.ShapeDtypeStruct((S * ROW_TILE, LANES), F32),
                   jax.ShapeDtypeStruct((S, LANES), jnp.int32), jax.ShapeDtypeStruct((S, LANES), F32)],
        compiler_params=_params("arbitrary"),
        name="out_router",
    )(x, o_ret, o_sb, o_fx, w_out3, g, w_hi, w_lo, b_rt)


ROW_TILE = D_MODEL // LANES
GATHER_BUFS = 3


def _store_row_tiles(ref, val, n):
    for c in range(ROW_TILE):
        ref[pl.ds(c, n, stride=ROW_TILE), :] = val[:, c * LANES:(c + 1) * LANES]


def _load_row_tiles(ref, first, n):
    return jnp.concatenate([ref[pl.ds(first * ROW_TILE + c, n, stride=ROW_TILE), :] for c in range(ROW_TILE)], axis=1)


def _tile_gather(src_hbm, dst, sem, idx_ref, base, n):
    def one(r):
        src_row = pl.multiple_of(idx_ref[base + r] * ROW_TILE, ROW_TILE)
        pltpu.make_async_copy(src_hbm.at[pl.ds(src_row, ROW_TILE), :], dst.at[pl.ds(r * ROW_TILE, ROW_TILE), :],
                              sem).start()

    def start_loop():
        lax.fori_loop(0, n, lambda r, c: (one(r), c)[1], 0, unroll=8)

    def start_unrolled(part=0, parts=1):
        for r in range(part * n // parts, (part + 1) * n // parts):
            one(r)

    def wait():
        pltpu.make_async_copy(src_hbm.at[pl.ds(0, n * ROW_TILE), :], dst, sem).wait()

    return start_loop, start_unrolled, wait


def _expert_kernel(be_ref, tok_ref, nused_ref, x_hbm, g_ref, wg_ref, wu_ref, wd_ref, y_ref,
                   xbuf, wg_bf, wu_bf, wd_bf, sem, *, rb):
    b = pl.program_id(0)
    nused = nused_ref[0]
    slot = b % GATHER_BUFS

    def gather(blk, s):
        return _tile_gather(x_hbm, xbuf.at[s], sem.at[s], tok_ref, jnp.minimum(blk, nused - 1) * rb, rb)

    @pl.when(b == 0)
    def _():
        for s in range(GATHER_BUFS - 1):
            gather(s, s)[0]()

    @pl.when(b < nused)
    def _():
        @pl.when(jnp.logical_or(b == 0, be_ref[b] != be_ref[jnp.maximum(b - 1, 0)]))
        def _():
            wg_bf[...] = wg_ref[0].astype(BF16)
            wu_bf[...] = wu_ref[0].astype(BF16)
            wd_bf[...] = wd_ref[0].astype(BF16)

        gather(b, slot)[2]()
        ahead = gather(b + GATHER_BUFS - 1, (b + GATHER_BUFS - 1) % GATHER_BUFS)[1]
        h = _rmsnorm(_load_row_tiles(xbuf.at[slot], 0, rb), g_ref[...]).astype(BF16)
        ahead(0, 4)
        gt = _dot(h, wg_bf[...])
        ahead(1, 4)
        up = _dot(h, wu_bf[...])
        ahead(2, 4)
        hid = gt * (1.0 / (1.0 + jnp.exp(-gt))) * up
        y = _dot(hid.astype(BF16), wd_bf[...])
        ahead(3, 4)
        _store_row_tiles(y_ref, y, rb)

    @pl.when(b == nused)
    def _():
        for s in range(GATHER_BUFS - 1):
            gather(b, (b + s) % GATHER_BUFS)[2]()

    @pl.when(b >= nused)
    def _():
        y_ref[...] = jnp.zeros_like(y_ref)


def _experts(blk_expert, tok_slot, nused, x1, g, w_gate, w_up, w_down, layer, rb):
    P = tok_slot.shape[0]
    nblk = P // rb
    live = lambda b, nu: jnp.minimum(b, nu[0] - 1)
    wsel = lambda b, be, tk, nu: (layer * N_EXPERTS + be[live(b, nu)], 0, 0)
    grid_spec = pltpu.PrefetchScalarGridSpec(
        num_scalar_prefetch=3,
        grid=(nblk,),
        in_specs=[pl.BlockSpec(memory_space=pl.ANY),
                  pl.BlockSpec((1, D_MODEL), lambda b, be, tk, nu: (0, 0)),
                  pl.BlockSpec((1, D_MODEL, D_EXPERT), wsel),
                  pl.BlockSpec((1, D_MODEL, D_EXPERT), wsel),
                  pl.BlockSpec((1, D_EXPERT, D_MODEL), wsel)],
        out_specs=pl.BlockSpec((rb * ROW_TILE, LANES), lambda b, be, tk, nu: (b, 0)),
        scratch_shapes=[pltpu.VMEM((GATHER_BUFS, rb * ROW_TILE, LANES), F32),
                        pltpu.VMEM((D_MODEL, D_EXPERT), BF16), pltpu.VMEM((D_MODEL, D_EXPERT), BF16),
                        pltpu.VMEM((D_EXPERT, D_MODEL), BF16), pltpu.SemaphoreType.DMA((GATHER_BUFS,))],
    )
    return pl.pallas_call(
        functools.partial(_expert_kernel, rb=rb),
        grid_spec=grid_spec,
        out_shape=jax.ShapeDtypeStruct((P * ROW_TILE, LANES), F32),
        compiler_params=_params("arbitrary"),
        name="experts",
    )(blk_expert, tok_slot, nused, x1, g, w_gate, w_up, w_down)


def _combine_kernel(dest_ref, x_ref, gate_ref, y_hbm, gf_ref, o_ref, ybuf, sem, *, tb, final):
    b = pl.program_id(0)
    nb = pl.num_programs(0)
    slot = b % 2

    def gather(blk, s):
        return _tile_gather(y_hbm, ybuf.at[s], sem.at[s], dest_ref, blk * (TOP_K * tb), TOP_K * tb)

    @pl.when(b == 0)
    def _():
        gather(0, 0)[0]()

    @pl.when(b + 1 < nb)
    def _():
        gather(b + 1, 1 - slot)[0]()

    gather(b, slot)[2]()
    gate = gate_ref[...]
    y0 = _load_row_tiles(ybuf.at[slot], 0, tb)
    y1 = _load_row_tiles(ybuf.at[slot], tb, tb)
    out = x_ref[...] + (gate[:, 0:1] * y0 + gate[:, 1:2] * y1)
    if final:
        out = _rmsnorm(out, gf_ref[...])
    o_ref[...] = out


def _combine(dest_blocked, x1, gates, ybuf, g_final, tb, final):
    S = x1.shape[0]
    grid_spec = pltpu.PrefetchScalarGridSpec(
        num_scalar_prefetch=1,
        grid=(S // tb,),
        in_specs=[pl.BlockSpec((tb, D_MODEL), lambda b, d: (b, 0)),
                  pl.BlockSpec((tb, LANES), lambda b, d: (b, 0)),
                  pl.BlockSpec(memory_space=pl.ANY),
                  pl.BlockSpec((1, D_MODEL), lambda b, d: (0, 0))],
        out_specs=pl.BlockSpec((tb, D_MODEL), lambda b, d: (b, 0)),
        scratch_shapes=[pltpu.VMEM((2, TOP_K * tb * ROW_TILE, LANES), F32), pltpu.SemaphoreType.DMA((2,))],
    )
    return pl.pallas_call(
        functools.partial(_combine_kernel, tb=tb, final=final),
        grid_spec=grid_spec,
        out_shape=jax.ShapeDtypeStruct((S, D_MODEL), F32),
        compiler_params=_params("arbitrary"),
        name="combine",
    )(dest_blocked, x1, gates, ybuf, g_final)


def _pad_cols(w, width):
    return jnp.pad(w, ((0, 0), (0, width - w.shape[1])))


def _proj_weights(w_in):
    w_ret, w_sb, w_fox = N_RET * HEAD_DIM, N_SB * HEAD_DIM, N_FOX * HEAD_DIM
    sizes = [w_ret] * 4 + [w_sb] * 3 + [w_fox] * 3 + [N_FOX]
    offs = np.concatenate([[0], np.cumsum(sizes)])
    q_r, k_r, v_r, gate_r, q_s, k_s, v_s, q_f, k_f, v_f, f_l = [w_in[:, offs[n]:offs[n + 1]] for n in range(11)]
    q_r = q_r * Q_SCALE
    segs = [q_r, k_r, v_r, gate_r,
            q_s * (Q_SCALE * LOG2E), k_s, v_s, q_f * (Q_SCALE * LOG2E), k_f, v_f]
    cols = [_pad_cols(s, GROUP_W) for s in segs] + [_pad_cols(f_l, LANES)]
    return jnp.concatenate(cols, axis=1).astype(BF16)


def _rope_tables(S):
    inv = 1.0 / (ROPE_BASE ** (jnp.arange(HALF, dtype=F32) / HALF))
    ang = jnp.arange(S).astype(F32)[:, None] * inv[None, :]
    cos, sin = jnp.cos(ang), jnp.sin(ang)
    return jnp.concatenate([cos] * 4, axis=1), jnp.concatenate([sin] * 4, axis=1)


def _slot_tokens_kernel(dest_ref, tok_ref):
    n_tok = dest_ref.shape[0] // TOP_K

    def fill(s, c):
        tok_ref[s] = lax.rem(s, n_tok)
        return c

    def put(a, c):
        tok_ref[dest_ref[a]] = lax.shift_right_logical(a, 1)
        return c

    lax.fori_loop(0, tok_ref.shape[0], fill, 0, unroll=8)
    lax.fori_loop(0, dest_ref.shape[0], put, 0, unroll=8)


def _slot_tokens(dest, n_slots):
    return pl.pallas_call(
        _slot_tokens_kernel,
        in_specs=[pl.BlockSpec(memory_space=pltpu.SMEM)],
        out_specs=pl.BlockSpec(memory_space=pltpu.SMEM),
        out_shape=jax.ShapeDtypeStruct((n_slots,), jnp.int32),
        name="slot_tokens",
    )(dest)


def _routing_tables(idx, rb):
    T = idx.shape[0]
    A = T * TOP_K
    P = A + N_EXPERTS * rb
    e = idx[:, :TOP_K].reshape(A)
    onehot = (e[:, None] == jnp.arange(N_EXPERTS, dtype=jnp.int32)[None, :]).astype(jnp.int32)
    cs = jnp.cumsum(onehot, axis=0)
    rank = jnp.sum(cs * onehot, axis=1) - 1
    counts = cs[-1]
    padded = ((counts + rb - 1) // rb) * rb
    pend = jnp.cumsum(padded)
    pstart = pend - padded
    dest = (jnp.sum(onehot * pstart[None, :], axis=1) + rank).astype(jnp.int32)
    tok_slot = _slot_tokens(dest, P)
    blk_start = jnp.arange(P // rb, dtype=jnp.int32) * rb
    blk_expert = jnp.minimum(jnp.sum((pend[None, :] <= blk_start[:, None]).astype(jnp.int32), axis=1),
                             N_EXPERTS - 1).astype(jnp.int32)
    nused = (pend[-1] // rb).astype(jnp.int32).reshape(1)
    return dest, tok_slot, blk_expert, nused


def kernel(x, norm_mix, w_in, b_forget, g_ret, g_sb, g_fox, w_out, norm_ffn, w_group, b_group,
           w_router, b_router, w_gate, w_up, w_down, norm_final):
    S = x.shape[1]
    ts = min(512, S)
    tq = min(256, S)
    rb = 256
    tb = min(256, S)
    depth = w_in.shape[0]
    xs = x.reshape(S, D_MODEL)
    cos_t, sin_t = _rope_tables(S)
    row1 = lambda v, width: jnp.pad(v, (0, width - v.shape[0])).reshape(1, width)
    nq = S // tq

    for l in range(depth):
        w_all = _proj_weights(w_in[l])
        qr, kr, vr, gate, qs, ks, vs, qf, kf, vf, fcum = _proj(
            xs, norm_mix[l].reshape(1, D_MODEL), w_all, cos_t, sin_t, row1(b_forget[l], LANES), ts)
        o_ret = _retention(qr, kr, vr, gate, row1(g_ret[l], GROUP_W), tq)
        o_sb = _stick_breaking(qs, ks, vs, row1(g_sb[l], GROUP_W), tq)
        f_pairs = (fcum[:, :2 * PAIRS] * LOG2E).T.reshape(PAIRS, 2, nq, tq).transpose(0, 2, 1, 3)
        o_fx = _fox(qf, kf, vf, f_pairs, row1(g_fox[l], GROUP_W), tq)

        w_o = w_out[l]
        w_ret, w_sb = N_RET * HEAD_DIM, N_SB * HEAD_DIM
        pad_rows = lambda w: jnp.pad(w, ((0, GROUP_W - w.shape[0]), (0, 0)))
        w_out3 = jnp.stack([pad_rows(w_o[:w_ret]), pad_rows(w_o[w_ret:w_ret + w_sb]),
                            pad_rows(w_o[w_ret + w_sb:])]).astype(BF16)
        w_rt = _pad_cols(jnp.concatenate([w_group[l], w_router[l]], axis=1), LANES)
        w_hi = w_rt.astype(BF16)
        w_lo = (w_rt - w_hi.astype(F32)).astype(BF16)
        b_rt = row1(jnp.concatenate([b_group[l], b_router[l]]), LANES)
        g_ffn = norm_ffn[l].reshape(1, D_MODEL)
        x1, x1_tiles, ridx, rgate = _out_router(xs, o_ret, o_sb, o_fx, w_out3, g_ffn, w_hi, w_lo, b_rt, ts)

        dest, tok_slot, blk_expert, nused = _routing_tables(ridx, rb)
        flat = lambda w: w.reshape((depth * N_EXPERTS,) + w.shape[2:])
        ybuf = _experts(blk_expert, tok_slot, nused, x1_tiles, g_ffn, flat(w_gate), flat(w_up), flat(w_down), l, rb)
        dest_blocked = dest.reshape(S // tb, tb, TOP_K).transpose(0, 2, 1).reshape(-1)
        xs = _combine(dest_blocked, x1, rgate, ybuf, norm_final.reshape(1, D_MODEL), tb, l == depth - 1)
    return xs.reshape(x.shape)
```
